```python
import math
import jax, jax.numpy as jnp
from jax import lax
import numpy as np

D_MODEL = 1024
BATCH = 4
SEQ = 8192
DEPTH = 1

D_MIX = D_MODEL
D_RWKV = D_MIX // 2
D_S5 = D_MIX - D_RWKV
RWKV_HEAD = 64
N_RWKV_HEADS = D_RWKV // RWKV_HEAD
LORA_W = 64
LORA_A = 64
LORA_G = 128
S5_CH = 16
N_S5_GROUPS = D_S5 // S5_CH
S5_STATE = 64
D_FF = ((8 * D_MODEL // 3 + 255) // 256) * 256
D_SHIFT = 3 * D_RWKV + LORA_W + LORA_A + LORA_G
D_IN = D_SHIFT + D_S5
NORM_EPS = 1e-6
LNX_EPS = 64e-5
DT_MIN = 1e-3
DT_MAX = 1e-1

kernel_name = "hybrid_rwkv7_s5_adaln_block"


def rms_norm(x, eps=NORM_EPS):
    xf = x.astype(jnp.float32)
    y = xf * lax.rsqrt(jnp.mean(xf * xf, axis=-1, keepdims=True) + eps)
    return y.astype(x.dtype)


def modulate(h, shift, scale):
    return h * (1.0 + scale[:, None, :]) + shift[:, None, :]


def rwkv7_time_mix(z, w0, w2, a0, a2, g2, k_k, k_a, r_k, lnx_w, lnx_b):
    out_dtype = z.dtype
    z = z.astype(jnp.float32)
    bsz, seq = z.shape[0], z.shape[1]
    H, N = N_RWKV_HEADS, RWKV_HEAD
    o = 0
    r = z[..., o:o + D_RWKV]; o += D_RWKV
    k = z[..., o:o + D_RWKV]; o += D_RWKV
    v = z[..., o:o + D_RWKV]; o += D_RWKV
    w_lo = z[..., o:o + LORA_W]; o += LORA_W
    a_lo = z[..., o:o + LORA_A]; o += LORA_A
    g_lo = z[..., o:o + LORA_G]

    w = -jax.nn.softplus(-(w0 + jnp.tanh(w_lo) @ w2)) - 0.5
    decay = jnp.exp(-jnp.exp(w))
    a = jax.nn.sigmoid(a0 + a_lo @ a2)
    g = jax.nn.sigmoid(g_lo) @ g2
    kk = (k * k_k).reshape(bsz, seq, H, N)
    kk = kk / jnp.maximum(jnp.linalg.norm(kk, axis=-1, keepdims=True), 1e-12)
    k = k * (1.0 + (a - 1.0) * k_a)

    heads = lambda t: t.reshape(bsz, seq, H, N)
    r_h, k_h, v_h, w_h, a_h = heads(r), heads(k), heads(v), heads(decay), heads(a)
    tm = lambda t: jnp.transpose(t, (1, 0, 2, 3))
    xs = (tm(r_h), tm(w_h), tm(k_h), tm(v_h), tm(-kk), tm(kk * a_h))

    def step(S, inp):
        r_t, w_t, k_t, v_t, a_t, b_t = inp
        sa = jnp.einsum("bhvk,bhk->bhv", S, a_t)
        S = S * w_t[:, :, None, :] + sa[..., None] * b_t[:, :, None, :] + v_t[..., None] * k_t[:, :, None, :]
        y = jnp.einsum("bhvk,bhk->bhv", S, r_t)
        return S, y

    S0 = jnp.zeros((bsz, H, N, N), jnp.float32)
    _, y = lax.scan(step, S0, xs)
    y = jnp.transpose(y, (1, 0, 2, 3))

    mu = jnp.mean(y, axis=-1, keepdims=True)
    var = jnp.mean(jnp.square(y - mu), axis=-1, keepdims=True)
    y = (y - mu) * lax.rsqrt(var + LNX_EPS) * lnx_w.reshape(H, N) + lnx_b.reshape(H, N)
    y = y + jnp.sum(r_h * k_h * r_k, axis=-1, keepdims=True) * v_h
    return (y.reshape(bsz, seq, D_RWKV) * g).astype(out_dtype)


def _ssm_combine(e1, e2):
    ar1, ai1, br1, bi1 = e1
    ar2, ai2, br2, bi2 = e2
    ar = ar2 * ar1 - ai2 * ai1
    ai = ar2 * ai1 + ai2 * ar1
    br = ar2 * br1 - ai2 * bi1 + br2
    bi = ar2 * bi1 + ai2 * br1 + bi2
    return ar, ai, br, bi


def s5_mix(u, a_re, a_im, log_dt, b_re, b_im, c_re, c_im, d_skip, w_glu, b_glu, gain):
    out_dtype = u.dtype
    u = u.astype(jnp.float32)
    bsz, seq = u.shape[0], u.shape[1]
    G, P = N_S5_GROUPS, S5_STATE
    ug = u.reshape(bsz, seq, G, S5_CH)

    dt = jnp.exp(log_dt)[:, None]
    mag = jnp.exp(dt * a_re)
    abar_re = mag * jnp.cos(dt * a_im)
    abar_im = mag * jnp.sin(dt * a_im)
    den = a_re * a_re + a_im * a_im
    p, q = abar_re - 1.0, abar_im
    coef_re = (p * a_re + q * a_im) / den
    coef_im = (q * a_re - p * a_im) / den
    bbar_re = coef_re[..., None] * b_re - coef_im[..., None] * b_im
    bbar_im = coef_re[..., None] * b_im + coef_im[..., None] * b_re

    bu_re = jnp.einsum("gpc,btgc->tbgp", bbar_re, ug)
    bu_im = jnp.einsum("gpc,btgc->tbgp", bbar_im, ug)
    a_seq_re = jnp.broadcast_to(abar_re[None, None], (seq, 1, G, P))
    a_seq_im = jnp.broadcast_to(abar_im[None, None], (seq, 1, G, P))
    _, _, s_re, s_im = lax.associative_scan(_ssm_combine, (a_seq_re, a_seq_im, bu_re, bu_im), axis=0)

    y = (jnp.einsum("gcp,tbgp->btgc", c_re, s_re) - jnp.einsum("gcp,tbgp->btgc", c_im, s_im)
         + d_skip * ug).reshape(bsz, seq, D_S5)
    zz = jax.nn.gelu(y)
    out = zz * jax.nn.sigmoid(zz @ w_glu + b_glu)
    return (rms_norm(out) * gain).astype(out_dtype)


def setup_inputs(seed: int = 0) -> dict:
    key = jax.random.key(seed)
    ks = jax.random.split(key, 40)
    L = DEPTH
    nrm = lambda k, shape, s: jax.random.normal(k, shape, jnp.float32) * s
    uni = lambda k, shape, lo, hi: jax.random.uniform(k, shape, jnp.float32, lo, hi)
    H, N, G, P = N_RWKV_HEADS, RWKV_HEAD, N_S5_GROUPS, S5_STATE
    return {
        "x": nrm(ks[0], (BATCH, SEQ, D_MODEL), 1.0),
        "c": nrm(ks[1], (BATCH, D_MODEL), 1.0),
        "w_ada": nrm(ks[2], (L, D_MODEL, 6 * D_MODEL), D_MODEL ** -0.5),
        "b_ada": nrm(ks[3], (L, 6 * D_MODEL), 0.01),
        "w_in": nrm(ks[4], (L, D_MODEL, D_IN), D_MODEL ** -0.5),
        "mu_shift": uni(ks[5], (L, D_SHIFT), 0.0, 1.0),
        "rw_w0": uni(ks[6], (L, D_RWKV), -6.0, 1.0),
        "rw_w2": nrm(ks[7], (L, LORA_W, D_RWKV), 0.5 * LORA_W ** -0.5),
        "rw_a0": nrm(ks[8], (L, D_RWKV), 0.1),
        "rw_a2": nrm(ks[9], (L, LORA_A, D_RWKV), 0.5 * LORA_A ** -0.5),
        "rw_g2": nrm(ks[10], (L, LORA_G, D_RWKV), LORA_G ** -0.5),
        "rw_k_k": 0.85 + nrm(ks[11], (L, D_RWKV), 0.05),
        "rw_k_a": 1.0 + nrm(ks[12], (L, D_RWKV), 0.05),
        "rw_r_k": nrm(ks[13], (L, H, N), 0.1),
        "rw_lnx_w": 1.0 + nrm(ks[14], (L, D_RWKV), 0.05),
        "rw_lnx_b": nrm(ks[15], (L, D_RWKV), 0.01),
        "s5_a_re": -0.5 + nrm(ks[16], (L, G, P), 0.01),
        "s5_a_im": math.pi * jnp.arange(P, dtype=jnp.float32)[None, None, :] + nrm(ks[17], (L, G, P), 0.01),
        "s5_log_dt": uni(ks[18], (L, G), math.log(DT_MIN), math.log(DT_MAX)),
        "s5_b_re": nrm(ks[19], (L, G, P, S5_CH), (2 * S5_CH) ** -0.5),
        "s5_b_im": nrm(ks[20], (L, G, P, S5_CH), (2 * S5_CH) ** -0.5),
        "s5_c_re": nrm(ks[21], (L, G, S5_CH, P), (2 * P) ** -0.5),
        "s5_c_im": nrm(ks[22], (L, G, S5_CH, P), (2 * P) ** -0.5),
        "s5_d": nrm(ks[23], (L, G, S5_CH), 1.0),
        "s5_w_glu": nrm(ks[24], (L, D_S5, D_S5), D_S5 ** -0.5),
        "s5_b_glu": nrm(ks[25], (L, D_S5), 0.01),
        "s5_gain": 1.0 + nrm(ks[26], (L, D_S5), 0.05),
        "w_out": nrm(ks[27], (L, D_MIX, D_MODEL), D_MIX ** -0.5),
        "ffn_w_gate": nrm(ks[28], (L, D_MODEL, D_FF), D_MODEL ** -0.5),
        "ffn_w_up": nrm(ks[29], (L, D_MODEL, D_FF), D_MODEL ** -0.5),
        "ffn_w_down": nrm(ks[30], (L, D_FF, D_MODEL), D_FF ** -0.5),
        "final_gain": 1.0 + nrm(ks[31], (D_MODEL,), 0.05),
    }


def reference(x, c, w_ada, b_ada, w_in, mu_shift, rw_w0, rw_w2, rw_a0, rw_a2, rw_g2,
              rw_k_k, rw_k_a, rw_r_k, rw_lnx_w, rw_lnx_b, s5_a_re, s5_a_im, s5_log_dt,
              s5_b_re, s5_b_im, s5_c_re, s5_c_im, s5_d, s5_w_glu, s5_b_glu, s5_gain,
              w_out, ffn_w_gate, ffn_w_up, ffn_w_down, final_gain):
    c_act = jax.nn.silu(c)
    for l in range(DEPTH):
        ada = c_act @ w_ada[l] + b_ada[l]
        sh_m, sc_m, g_m, sh_f, sc_f, g_f = jnp.split(ada, 6, axis=-1)

        h = modulate(rms_norm(x), sh_m, sc_m)
        proj = h @ w_in[l]
        z = proj[..., :D_SHIFT]
        u = proj[..., D_SHIFT:]
        z_prev = jnp.pad(z[:, :-1], ((0, 0), (1, 0), (0, 0)))
        z = z + mu_shift[l] * (z_prev - z)
        y_rwkv = rwkv7_time_mix(z, rw_w0[l], rw_w2[l], rw_a0[l], rw_a2[l], rw_g2[l],
                                rw_k_k[l], rw_k_a[l], rw_r_k[l], rw_lnx_w[l], rw_lnx_b[l])
        y_s5 = s5_mix(u, s5_a_re[l], s5_a_im[l], s5_log_dt[l], s5_b_re[l], s5_b_im[l],
                      s5_c_re[l], s5_c_im[l], s5_d[l], s5_w_glu[l], s5_b_glu[l], s5_gain[l])
        mix = jnp.concatenate([y_rwkv, y_s5], axis=-1) @ w_out[l]
        x = x + g_m[:, None, :] * mix

        h = modulate(rms_norm(x), sh_f, sc_f)
        ffn = (jax.nn.silu(h @ ffn_w_gate[l]) * (h @ ffn_w_up[l])) @ ffn_w_down[l]
        x = x + g_f[:, None, :] * ffn
    return rms_norm(x) * final_gain
```

```python
import functools
import math

import jax
import jax.numpy as jnp
from jax import lax
from jax.experimental import pallas as pl
from jax.experimental.pallas import tpu as pltpu

F32 = jnp.float32
BF16 = jnp.bfloat16

D_MODEL = 1024
D_RWKV = 512
D_S5 = 512
HEAD = 64
LORA_W = 64
LORA_A = 64
LORA_G = 128
S5_CH = 16
S5_GROUPS = 32
S5_STATE = 64
D_FF = 2816
D_SHIFT = 3 * D_RWKV + LORA_W + LORA_A + LORA_G
D_IN = D_SHIFT + D_S5
NORM_EPS = 1e-6
LNX_EPS = 64e-5

CHUNK = 64
PAIR = 2 * HEAD
S5_L = 16
VMEM_LIMIT = 56 * 1024 * 1024


def _split_bf16(x, n):
    parts = []
    rem = x
    for i in range(n):
        p = rem.astype(BF16)
        parts.append(p)
        if i + 1 < n:
            rem = rem - p.astype(F32)
    return parts


def _dot(a, b):
    return jnp.dot(a, b, preferred_element_type=F32)


def _dot_nt(a, b):
    return lax.dot_general(a, b, (((1,), (1,)), ((), ())), preferred_element_type=F32)


def _dot_split_lhs(x, rhs_bf16, n):
    acc = None
    for p in _split_bf16(x, n):
        d = _dot(p, rhs_bf16)
        acc = d if acc is None else acc + d
    return acc


def _sigmoid(x):
    return 1.0 / (1.0 + jnp.exp(-x))


def _ada_kernel(c_ref, w_ref, b_ref, o_ref):
    c = c_ref[...]
    act = c * _sigmoid(c)
    o_ref[...] = jnp.dot(act, w_ref[...], preferred_element_type=F32,
                         precision=lax.Precision.HIGHEST) + b_ref[...]


def _ada(c, w_ada, b_ada):
    bsz, d = c.shape
    rows = 8
    c_pad = jnp.zeros((rows, d), F32).at[:bsz].set(c)
    n_out = w_ada.shape[1]
    out = pl.pallas_call(
        _ada_kernel,
        grid=(n_out // d,),
        in_specs=[
            pl.BlockSpec((rows, d), lambda j: (0, 0)),
            pl.BlockSpec((d, d), lambda j: (0, j)),
            pl.BlockSpec((1, d), lambda j: (0, j)),
        ],
        out_specs=pl.BlockSpec((rows, d), lambda j: (0, j)),
        out_shape=jax.ShapeDtypeStruct((rows, n_out), F32),
        compiler_params=pltpu.CompilerParams(dimension_semantics=("arbitrary",)),
    )(c_pad, w_ada, b_ada.reshape(1, n_out))
    return out[:bsz]


def _inproj_kernel(x_ref, sh_ref, sc_ref, win_ref, mu_ref, w0_ref, w2_ref, a0_ref, a2_ref, g2_ref,
                   kk_ref, ka_ref, bd_ref,
                   r_out, k_out, v_out, ld_out, a_out, b_out, g_out, u_out, carry_ref):
    t = pl.program_id(1)
    x = x_ref[...]
    tm = x.shape[0]
    ms = jnp.mean(x * x, axis=-1, keepdims=True)
    h = x * lax.rsqrt(ms + NORM_EPS)
    h = h * (1.0 + sc_ref[...]) + sh_ref[...]
    proj = _dot(h.astype(BF16), win_ref[...])
    z = proj[:, :D_SHIFT]
    u_out[...] = proj[:, D_SHIFT:]

    @pl.when(t == 0)
    def _():
        carry_ref[...] = jnp.zeros_like(carry_ref)

    z_roll = pltpu.roll(z, 1, axis=0)
    row = lax.broadcasted_iota(jnp.int32, z.shape, 0)
    z_prev = jnp.where(row == 0, carry_ref[...], z_roll)
    carry_ref[...] = z[tm - 1:tm, :]
    zz = z + mu_ref[...] * (z_prev - z)

    r = zz[:, 0:D_RWKV]
    k = zz[:, D_RWKV:2 * D_RWKV]
    v = zz[:, 2 * D_RWKV:3 * D_RWKV]
    o = 3 * D_RWKV
    w_lo = zz[:, o:o + LORA_W]
    a_lo = zz[:, o + LORA_W:o + LORA_W + LORA_A]
    g_lo = zz[:, o + LORA_W + LORA_A:o + LORA_W + LORA_A + LORA_G]

    wl = w0_ref[...] + _dot(jnp.tanh(w_lo).astype(BF16), w2_ref[...])
    sp = jnp.maximum(-wl, 0.0) + jnp.log(1.0 + jnp.exp(-jnp.abs(wl)))
    ld_out[...] = -jnp.exp(-sp - 0.5)
    a = _sigmoid(a0_ref[...] + _dot(a_lo.astype(BF16), a2_ref[...]))
    g_out[...] = _dot(_sigmoid(g_lo).astype(BF16), g2_ref[...])

    kk = k * kk_ref[...]
    n2 = _dot_split_lhs(kk * kk, bd_ref[...], 2)
    kk = kk / jnp.maximum(jnp.sqrt(n2), 1e-12)
    r_out[...] = r
    k_out[...] = k * (1.0 + (a - 1.0) * ka_ref[...])
    v_out[...] = v
    a_out[...] = -kk
    b_out[...] = kk * a


def _head_block_diag(n, dtype, value=1.0):
    i = jnp.arange(n) // HEAD
    return jnp.where(i[:, None] == i[None, :], value, 0.0).astype(dtype)


def _inproj(x, sh_m, sc_m, w_in, mu, w0, w2, a0, a2, g2, k_k, k_a, tm):
    bsz, seq, d = x.shape
    row = lambda a: a.reshape(1, -1)
    const = lambda shape: pl.BlockSpec(shape, lambda b, t: (0,) * len(shape))
    tok = lambda n: pl.BlockSpec((None, tm, n), lambda b, t: (b, t, 0))
    per_b = pl.BlockSpec((None, 1, d), lambda b, t: (b, 0, 0))
    out_sds = jax.ShapeDtypeStruct((bsz, seq, D_RWKV), F32)
    return pl.pallas_call(
        _inproj_kernel,
        grid=(bsz, seq // tm),
        in_specs=[tok(d), per_b, per_b, const((d, D_IN)), const((1, D_SHIFT)),
                  const((1, D_RWKV)), const((LORA_W, D_RWKV)), const((1, D_RWKV)),
                  const((LORA_A, D_RWKV)), const((LORA_G, D_RWKV)), const((1, D_RWKV)),
                  const((1, D_RWKV)), const((D_RWKV, D_RWKV))],
        out_specs=[tok(D_RWKV)] * 8,
        out_shape=[out_sds] * 8,
        scratch_shapes=[pltpu.VMEM((1, D_SHIFT), F32)],
        compiler_params=pltpu.CompilerParams(dimension_semantics=("parallel", "arbitrary"),
                                             vmem_limit_bytes=VMEM_LIMIT),
    )(x, sh_m.reshape(bsz, 1, d), sc_m.reshape(bsz, 1, d), w_in.astype(BF16), row(mu),
      row(w0), w2.astype(BF16), row(a0), a2.astype(BF16), g2.astype(BF16), row(k_k), row(k_a),
      _head_block_diag(D_RWKV, BF16))


def _rwkv_pair_chunk(r, k, v, ld, a, b, s_t, cst):
    tri, strict, incl, bd, eye, m0, m1, m0w, m1w, eye_c, merge_masks = cst
    c = r.shape[0]
    cum = None
    for p in _split_bf16(ld, 3):
        d = _dot(tri, p)
        cum = d if cum is None else cum + d
    cl = cum[c - 1:c, :]
    w_to = jnp.exp(cum)
    w_inv = jnp.exp(-cum)
    w_prev = jnp.exp(cum - ld)
    w_rem = jnp.exp(cl - cum)
    w_all = jnp.exp(cl)
    rt = r * w_to
    kt = k * w_inv
    at = a * w_prev
    bt = b * w_inv
    bh = b * w_rem
    kh = k * w_rem

    lhs = jnp.concatenate([at, rt], axis=0).astype(BF16)
    rhs = jnp.concatenate([bt * m0, bt * m1, kt * m0, kt * m1], axis=0).astype(BF16)
    a_all = _dot_nt(lhs, rhs)
    n_ab = jnp.where(strict, a_all[:c, :PAIR], 0.0)
    a_ak = jnp.where(strict, a_all[:c, PAIR:], 0.0)
    a_rb = jnp.where(incl, a_all[c:, :PAIR], 0.0)
    a_rk = jnp.where(incl, a_all[c:, PAIR:], 0.0)

    vs = jnp.concatenate([v * m0, v * m1], axis=0).astype(BF16)
    x = jnp.concatenate([at, _dot(a_ak.astype(BF16), vs)], axis=1)

    def stack(y):
        return jnp.concatenate([y * m0w, y * m1w], axis=0).astype(BF16)

    def block_diag(y):
        return (jnp.concatenate([y, y], axis=0) * bd).astype(BF16)

    t_inv = eye_c
    for msk in merge_masks:
        n_off = jnp.where(msk, n_ab, 0.0)
        tmp = _dot(t_inv.astype(BF16), block_diag(n_off))
        t_inv = t_inv + _dot(tmp.astype(BF16), block_diag(t_inv))
    x = _dot(t_inv.astype(BF16), stack(x))
    xs = stack(x)
    rx = _dot(a_rb.astype(BF16), xs)
    r_hat = rt + rx[:, :PAIR]
    y_hat = rx[:, PAIR:] + _dot(a_rk.astype(BF16), vs)
    g1 = _dot(bh.T.astype(BF16), x.astype(BF16))
    g2 = _dot(kh.T.astype(BF16), v.astype(BF16))
    m_mat = eye * w_all + g1[:, :PAIR] * bd
    s_hat = (g1[:, PAIR:] + g2) * bd

    sb = s_t.astype(BF16)
    y = _dot(r_hat.astype(BF16), sb) + y_hat
    s_new = _dot(m_mat.astype(BF16), sb) + s_hat
    return y, s_new


def _rwkv_kernel(r_ref, k_ref, v_ref, ld_ref, a_ref, b_ref, g_ref, rk_ref, lw_ref, lb_ref,
                 tri_ref, bd_ref, avg_ref, y_ref, s_ref):
    t = pl.program_id(1)

    @pl.when(t == 0)
    def _():
        s_ref[...] = jnp.zeros_like(s_ref)

    c = CHUNK
    lane = lax.broadcasted_iota(jnp.int32, (1, PAIR), 1)
    m0 = (lane < HEAD).astype(F32)
    m1 = 1.0 - m0
    lane_w = lax.broadcasted_iota(jnp.int32, (1, 2 * PAIR), 1) % PAIR
    m0w = (lane_w < HEAD).astype(F32)
    m1w = 1.0 - m0w
    row = lax.broadcasted_iota(jnp.int32, (c, PAIR), 0)
    col = lax.broadcasted_iota(jnp.int32, (c, PAIR), 1) % c
    strict = col < row
    incl = col <= row
    bd = bd_ref[...]
    ri = lax.broadcasted_iota(jnp.int32, (PAIR, PAIR), 0)
    ci = lax.broadcasted_iota(jnp.int32, (PAIR, PAIR), 1)
    eye = (ri == ci).astype(F32)
    eye_c = (col == row).astype(F32)
    merge_masks = []
    sz = 1
    while sz < c:
        merge_masks.append((row // (2 * sz) == col // (2 * sz)) & (row % (2 * sz) >= sz) & (col % (2 * sz) < sz))
        sz *= 2
    cst = (tri_ref[...], strict, incl, bd, eye, m0, m1, m0w, m1w, eye_c, merge_masks)
    avg = avg_ref[...]
    ones_bd = bd.astype(BF16)

    n_chunks = r_ref.shape[0] // c
    for p in range(D_RWKV // PAIR):
        ls = slice(p * PAIR, (p + 1) * PAIR)
        s_t = s_ref[p]
        for ch in range(n_chunks):
            rs = slice(ch * c, (ch + 1) * c)
            r = r_ref[rs, ls]
            k = k_ref[rs, ls]
            v = v_ref[rs, ls]
            y, s_t = _rwkv_pair_chunk(r, k, v, ld_ref[rs, ls], a_ref[rs, ls], b_ref[rs, ls], s_t, cst)
            mu = _dot_split_lhs(y, avg, 2)
            dlt = y - mu
            var = _dot_split_lhs(dlt * dlt, avg, 2)
            yn = dlt * lax.rsqrt(var + LNX_EPS) * lw_ref[:, ls] + lb_ref[:, ls]
            bonus = _dot_split_lhs(r * k * rk_ref[:, ls], ones_bd, 2)
            y_ref[rs, ls] = (yn + bonus * v) * g_ref[rs, ls]
        s_ref[p] = s_t


def _rwkv(r, k, v, ld, a, b, g, r_k, lnx_w, lnx_b, tb):
    bsz, seq, d = r.shape
    tok = pl.BlockSpec((None, tb, d), lambda i, t: (i, t, 0))
    const = lambda shape: pl.BlockSpec(shape, lambda i, t: (0,) * len(shape))
    tri = jnp.tril(jnp.ones((CHUNK, CHUNK), F32)).astype(BF16)
    return pl.pallas_call(
        _rwkv_kernel,
        grid=(bsz, seq // tb),
        in_specs=[tok] * 7 + [const((1, d))] * 3 + [const((CHUNK, CHUNK)), const((PAIR, PAIR)), const((PAIR, PAIR))],
        out_specs=tok,
        out_shape=jax.ShapeDtypeStruct((bsz, seq, d), F32),
        scratch_shapes=[pltpu.VMEM((d // PAIR, PAIR, PAIR), F32)],
        compiler_params=pltpu.CompilerParams(dimension_semantics=("parallel", "arbitrary"),
                                             vmem_limit_bytes=VMEM_LIMIT),
    )(r, k, v, ld, a, b, g, r_k.reshape(1, d), lnx_w.reshape(1, d), lnx_b.reshape(1, d),
      tri, _head_block_diag(PAIR, F32), _head_block_diag(PAIR, BF16, 1.0 / HEAD))


def _s5_prep_kernel(are_ref, aim_ref, ldt_ref, btre_ref, btim_ref, cre_ref, cim_ref,
                    x_out, p_out, k_out, lam_out):
    a_re = are_ref[...]
    a_im = aim_ref[...]
    dt = jnp.exp(ldt_ref[...])
    nd = 24
    dpow = lax.broadcasted_iota(jnp.int32, (nd, S5_STATE), 0).astype(F32)
    mag = jnp.exp(dpow * (dt * a_re))
    ang = dpow * (dt * a_im)
    e_re = mag * jnp.cos(ang)
    e_im = mag * jnp.sin(ang)
    lam_re = e_re[1:2, :]
    lam_im = e_im[1:2, :]
    den = a_re * a_re + a_im * a_im
    pp = lam_re - 1.0
    qq = lam_im
    coef_re = (pp * a_re + qq * a_im) / den
    coef_im = (qq * a_re - pp * a_im) / den
    bt_re = btre_ref[...]
    bt_im = btim_ref[...]
    bb_re = coef_re * bt_re - coef_im * bt_im
    bb_im = coef_re * bt_im + coef_im * bt_re
    c_re = cre_ref[...]
    c_im = cim_ref[...]
    for d in range(S5_L + 1):
        er = e_re[d:d + 1, :]
        ei = e_im[d:d + 1, :]
        x_out[d * S5_CH:(d + 1) * S5_CH, :] = jnp.concatenate(
            [c_re * er - c_im * ei, -(c_re * ei + c_im * er)], axis=1)
    for j in range(S5_L):
        d = S5_L - 1 - j
        er = e_re[d:d + 1, :]
        ei = e_im[d:d + 1, :]
        p_out[j * S5_CH:(j + 1) * S5_CH, :] = jnp.concatenate(
            [er * bb_re - ei * bb_im, er * bb_im + ei * bb_re], axis=1)
    y0 = jnp.concatenate([bb_re, bb_im], axis=1)
    k_out[...] = lax.dot_general(x_out[...], y0, (((1,), (1,)), ((), ())),
                                 preferred_element_type=F32, precision=lax.Precision.HIGHEST)
    lam_out[...] = jnp.concatenate([e_re[S5_L:S5_L + 1, :], e_im[S5_L:S5_L + 1, :]], axis=1)


def _s5_prep(a_re, a_im, log_dt, b_re, b_im, c_re, c_im):
    g, p = a_re.shape
    per_g = lambda shape: pl.BlockSpec((None,) + shape, lambda i: (i,) + (0,) * len(shape))
    nx = (S5_L + 1) * S5_CH
    return pl.pallas_call(
        _s5_prep_kernel,
        grid=(g,),
        in_specs=[per_g((1, p)), per_g((1, p)), per_g((1, 1)), per_g((S5_CH, p)), per_g((S5_CH, p)),
                  per_g((S5_CH, p)), per_g((S5_CH, p))],
        out_specs=[per_g((nx, 2 * p)), per_g((S5_L * S5_CH, 2 * p)), per_g((nx, S5_CH)), per_g((1, 2 * p))],
        out_shape=[jax.ShapeDtypeStruct((g, nx, 2 * p), F32),
                   jax.ShapeDtypeStruct((g, S5_L * S5_CH, 2 * p), F32),
                   jax.ShapeDtypeStruct((g, nx, S5_CH), F32),
                   jax.ShapeDtypeStruct((g, 1, 2 * p), F32)],
        compiler_params=pltpu.CompilerParams(dimension_semantics=("arbitrary",)),
    )(a_re.reshape(g, 1, p), a_im.reshape(g, 1, p), log_dt.reshape(g, 1, 1),
      jnp.swapaxes(b_re, 1, 2), jnp.swapaxes(b_im, 1, 2), c_re, c_im)


def _s5_local_kernel(u_ref, p_ref, z_ref):
    z_ref[...] = _dot(u_ref[...].astype(BF16), p_ref[...])


def _s5_scan_kernel(z_ref, lr_ref, li_ref, s_out, s_ref):
    @pl.when(pl.program_id(0) == 0)
    def _():
        s_ref[...] = jnp.zeros_like(s_ref)

    lr = lr_ref[...]
    li = li_ref[...]

    def body(i, s):
        s_out[i] = s
        return lr * s + li * pltpu.roll(s, S5_STATE, axis=1) + z_ref[i]

    s_ref[...] = lax.fori_loop(0, z_ref.shape[0], body, s_ref[...])


def _s5_out_kernel(u_ref, s_ref, toep_ref, qt_ref, dt_ref, y_ref):
    u = u_ref[...]
    y = _dot(u.astype(BF16), toep_ref[...])
    y = y + _dot_nt(s_ref[...].astype(BF16), qt_ref[...])
    y_ref[...] = y + u * dt_ref[...]


def _s5_core(u, a_re, a_im, log_dt, b_re, b_im, c_re, c_im, d_skip, scan_tile):
    bsz, seq, _ = u.shape
    g, ch, p, ll = S5_GROUPS, S5_CH, S5_STATE, S5_L
    nck = seq // ll
    n = nck * bsz
    w = ll * ch
    x_all, p_all, k_all, lam = _s5_prep(a_re, a_im, log_dt, b_re, b_im, c_re, c_im)

    qt = x_all[:, ch:, :].astype(BF16)
    p_mat = p_all.astype(BF16)
    kd = k_all.reshape(g, ll + 1, ch, ch)
    jj = jnp.arange(ll)
    lag = jj[None, :] - jj[:, None]
    toep = jnp.where((lag >= 0)[None, :, :, None, None], kd[:, jnp.clip(lag, 0, ll)], 0.0)
    toep = jnp.transpose(toep, (0, 1, 4, 2, 3)).reshape(g, w, w).astype(BF16)
    lam_re = jnp.broadcast_to(lam[:, :, :p], (g, bsz, p))
    lam_im = jnp.broadcast_to(lam[:, :, p:], (g, bsz, p))
    lr = jnp.concatenate([lam_re, lam_re], axis=-1).reshape(g * bsz, 2 * p)
    li = jnp.concatenate([-lam_im, lam_im], axis=-1).reshape(g * bsz, 2 * p)
    d_tile = jnp.tile(d_skip, (1, ll)).reshape(g, 1, w)

    ur = u.reshape(bsz, nck, ll, g, ch).transpose(3, 1, 0, 2, 4).reshape(g, n, w)

    per_g = lambda shape: pl.BlockSpec((None,) + shape, lambda i: (i,) + (0,) * len(shape))
    z = pl.pallas_call(
        _s5_local_kernel,
        grid=(g,),
        in_specs=[per_g((n, w)), per_g((w, 2 * p))],
        out_specs=per_g((n, 2 * p)),
        out_shape=jax.ShapeDtypeStruct((g, n, 2 * p), F32),
        compiler_params=pltpu.CompilerParams(dimension_semantics=("parallel",)),
    )(ur, p_mat)

    zs = z.reshape(g, nck, bsz, 2 * p).transpose(1, 0, 2, 3).reshape(nck, g * bsz, 2 * p)
    s_start = pl.pallas_call(
        _s5_scan_kernel,
        grid=(nck // scan_tile,),
        in_specs=[pl.BlockSpec((scan_tile, g * bsz, 2 * p), lambda i: (i, 0, 0)),
                  pl.BlockSpec((g * bsz, 2 * p), lambda i: (0, 0)),
                  pl.BlockSpec((g * bsz, 2 * p), lambda i: (0, 0))],
        out_specs=pl.BlockSpec((scan_tile, g * bsz, 2 * p), lambda i: (i, 0, 0)),
        out_shape=jax.ShapeDtypeStruct((nck, g * bsz, 2 * p), F32),
        scratch_shapes=[pltpu.VMEM((g * bsz, 2 * p), F32)],
        compiler_params=pltpu.CompilerParams(dimension_semantics=("arbitrary",)),
    )(zs, lr, li)
    sr = s_start.reshape(nck, g, bsz, 2 * p).transpose(1, 0, 2, 3).reshape(g, n, 2 * p)

    yr = pl.pallas_call(
        _s5_out_kernel,
        grid=(g,),
        in_specs=[per_g((n, w)), per_g((n, 2 * p)), per_g((w, w)), per_g((w, 2 * p)), per_g((1, w))],
        out_specs=per_g((n, w)),
        out_shape=jax.ShapeDtypeStruct((g, n, w), F32),
        compiler_params=pltpu.CompilerParams(dimension_semantics=("parallel",)),
    )(ur, sr, toep, qt, d_tile)
    return yr.reshape(g, nck, bsz, ll, ch).transpose(2, 1, 3, 0, 4).reshape(bsz, seq, g * ch)


def _tail_kernel(x_ref, yr_ref, ys_ref, gm_ref, shf_ref, scf_ref, gf_ref, wglu_ref, bglu_ref, gain_ref,
                 wo_ref, wg_ref, wu_ref, wd_ref, fg_ref, o_ref, *, ff_tile):
    ys = ys_ref[...]
    zz = 0.5 * ys * (1.0 + jnp.tanh(math.sqrt(2.0 / math.pi) * (ys + 0.044715 * (ys * ys * ys))))
    gl = zz * _sigmoid(_dot(zz.astype(BF16), wglu_ref[...]) + bglu_ref[...])
    gl = gl * lax.rsqrt(jnp.mean(gl * gl, axis=-1, keepdims=True) + NORM_EPS) * gain_ref[...]
    mix = _dot(yr_ref[...].astype(BF16), wo_ref[:D_RWKV, :]) + _dot(gl.astype(BF16), wo_ref[D_RWKV:, :])
    x1 = x_ref[...] + gm_ref[...] * mix
    h = x1 * lax.rsqrt(jnp.mean(x1 * x1, axis=-1, keepdims=True) + NORM_EPS)
    h = (h * (1.0 + scf_ref[...]) + shf_ref[...]).astype(BF16)
    acc = jnp.zeros_like(x1)
    for j in range(D_FF // ff_tile):
        cs = slice(j * ff_tile, (j + 1) * ff_tile)
        gate = _dot(h, wg_ref[:, cs])
        up = _dot(h, wu_ref[:, cs])
        act = gate * _sigmoid(gate) * up
        acc = acc + _dot(act.astype(BF16), wd_ref[cs, :])
    x2 = x1 + gf_ref[...] * acc
    o_ref[...] = x2 * lax.rsqrt(jnp.mean(x2 * x2, axis=-1, keepdims=True) + NORM_EPS) * fg_ref[...]


def _tail(x, y_rwkv, y_s5, g_m, sh_f, sc_f, g_f, w_glu, b_glu, gain, w_out, w_gate, w_up, w_down,
          final_gain, tm, ff_tile):
    bsz, seq, d = x.shape
    once = dict(pipeline_mode=pl.Buffered(1))
    const = lambda shape: pl.BlockSpec(shape, lambda b, t: (0,) * len(shape), **once)
    tok = lambda n: pl.BlockSpec((None, tm, n), lambda b, t: (b, t, 0))
    per_b = pl.BlockSpec((None, 1, d), lambda b, t: (b, 0, 0))
    b3 = lambda a: a.reshape(bsz, 1, d)
    return pl.pallas_call(
        functools.partial(_tail_kernel, ff_tile=ff_tile),
        grid=(bsz, seq // tm),
        in_specs=[tok(d), tok(D_RWKV), tok(D_S5), per_b, per_b, per_b, per_b,
                  const((D_S5, D_S5)), const((1, D_S5)), const((1, D_S5)),
                  const((D_RWKV + D_S5, d)), const((d, D_FF)), const((d, D_FF)), const((D_FF, d)),
                  const((1, d))],
        out_specs=tok(d),
        out_shape=jax.ShapeDtypeStruct((bsz, seq, d), F32),
        compiler_params=pltpu.CompilerParams(dimension_semantics=("parallel", "parallel"),
                                             vmem_limit_bytes=VMEM_LIMIT),
    )(x, y_rwkv, y_s5, b3(g_m), b3(sh_f), b3(sc_f), b3(g_f), w_glu.astype(BF16), b_glu.reshape(1, -1),
      gain.reshape(1, -1), w_out.astype(BF16), w_gate.astype(BF16), w_up.astype(BF16),
      w_down.astype(BF16), final_gain.reshape(1, d))


def kernel(x, c, w_ada, b_ada, w_in, mu_shift, rw_w0, rw_w2, rw_a0, rw_a2, rw_g2, rw_k_k, rw_k_a, rw_r_k,
           rw_lnx_w, rw_lnx_b, s5_a_re, s5_a_im, s5_log_dt, s5_b_re, s5_b_im, s5_c_re, s5_c_im, s5_d,
           s5_w_glu, s5_b_glu, s5_gain, w_out, ffn_w_gate, ffn_w_up, ffn_w_down, final_gain):
    assert w_ada.shape[0] == 1, "single-layer trunk"
    seq = x.shape[1]
    ada = _ada(c, w_ada[0], b_ada[0])
    sh_m, sc_m, g_m, sh_f, sc_f, g_f = jnp.split(ada, 6, axis=-1)
    r, k, v, ld, a, b, g, u = _inproj(x, sh_m, sc_m, w_in[0], mu_shift[0], rw_w0[0], rw_w2[0], rw_a0[0],
                                      rw_a2[0], rw_g2[0], rw_k_k[0], rw_k_a[0], tm=min(256, seq))
    y_rwkv = _rwkv(r, k, v, ld, a, b, g, rw_r_k[0], rw_lnx_w[0], rw_lnx_b[0], tb=CHUNK)
    y_s5 = _s5_core(u, s5_a_re[0], s5_a_im[0], s5_log_dt[0], s5_b_re[0], s5_b_im[0], s5_c_re[0],
                    s5_c_im[0], s5_d[0], scan_tile=min(64, seq // S5_L))
    return _tail(x, y_rwkv, y_s5, g_m, sh_f, sc_f, g_f, s5_w_glu[0], s5_b_glu[0], s5_gain[0], w_out[0],
                 ffn_w_gate[0], ffn_w_up[0], ffn_w_down[0], final_gain, tm=min(512, seq), ff_tile=256)
```

```python
import functools
import math

import jax
import jax.numpy as jnp
from jax import lax
from jax.experimental import pallas as pl
from jax.experimental.pallas import tpu as pltpu

F32 = jnp.float32
BF16 = jnp.bfloat16

D_MODEL = 1024
D_RWKV = 512
D_S5 = 512
HEAD = 64
LORA_W = 64
LORA_A = 64
LORA_G = 128
S5_CH = 16
S5_GROUPS = 32
S5_STATE = 64
D_FF = 2816
D_SHIFT = 3 * D_RWKV + LORA_W + LORA_A + LORA_G
D_IN = D_SHIFT + D_S5
NORM_EPS = 1e-6
LNX_EPS = 64e-5

CHUNK = 64
PAIR = 2 * HEAD
UNIT_CHUNKS = 2
S5_L = 16
VMEM_LIMIT = 56 * 1024 * 1024


def _split_bf16(x, n):
    parts = []
    rem = x
    for i in range(n):
        p = rem.astype(BF16)
        parts.append(p)
        if i + 1 < n:
            rem = rem - p.astype(F32)
    return parts


def _dot(a, b):
    return jnp.dot(a, b, preferred_element_type=F32)


def _dot_nt(a, b):
    return lax.dot_general(a, b, (((1,), (1,)), ((), ())), preferred_element_type=F32)


def _dot_split_lhs(x, rhs_bf16, n):
    acc = None
    for p in _split_bf16(x, n):
        d = _dot(p, rhs_bf16)
        acc = d if acc is None else acc + d
    return acc


def _sigmoid(x):
    return 1.0 / (1.0 + jnp.exp(-x))


def _ada_kernel(c_ref, w_ref, b_ref, o_ref):
    c = c_ref[...]
    act = c * _sigmoid(c)
    o_ref[...] = jnp.dot(act, w_ref[...], preferred_element_type=F32,
                         precision=lax.Precision.HIGHEST) + b_ref[...]


def _ada(c, w_ada, b_ada):
    bsz, d = c.shape
    rows = 8
    c_pad = jnp.zeros((rows, d), F32).at[:bsz].set(c)
    n_out = w_ada.shape[1]
    out = pl.pallas_call(
        _ada_kernel,
        grid=(n_out // d,),
        in_specs=[
            pl.BlockSpec((rows, d), lambda j: (0, 0)),
            pl.BlockSpec((d, d), lambda j: (0, j)),
            pl.BlockSpec((1, d), lambda j: (0, j)),
        ],
        out_specs=pl.BlockSpec((rows, d), lambda j: (0, j)),
        out_shape=jax.ShapeDtypeStruct((rows, n_out), F32),
        compiler_params=pltpu.CompilerParams(dimension_semantics=("arbitrary",)),
    )(c_pad, w_ada, b_ada.reshape(1, n_out))
    return out[:bsz]


def _inproj_kernel(x_ref, sh_ref, sc_ref, win_ref, mu_ref, w0_ref, w2_ref, a0_ref, a2_ref, g2_ref,
                   kk_ref, ka_ref, bd_ref,
                   r_out, k_out, v_out, ld_out, a_out, b_out, g_out, u_out, carry_ref):
    t = pl.program_id(1)
    x = x_ref[...]
    tm = x.shape[0]
    ms = jnp.mean(x * x, axis=-1, keepdims=True)
    h = x * lax.rsqrt(ms + NORM_EPS)
    h = h * (1.0 + sc_ref[...]) + sh_ref[...]
    proj = _dot(h.astype(BF16), win_ref[...])
    z = proj[:, :D_SHIFT]
    u_out[...] = proj[:, D_SHIFT:]

    @pl.when(t == 0)
    def _():
        carry_ref[...] = jnp.zeros_like(carry_ref)

    z_roll = pltpu.roll(z, 1, axis=0)
    row = lax.broadcasted_iota(jnp.int32, z.shape, 0)
    z_prev = jnp.where(row == 0, carry_ref[...], z_roll)
    carry_ref[...] = z[tm - 1:tm, :]
    zz = z + mu_ref[...] * (z_prev - z)

    r = zz[:, 0:D_RWKV]
    k = zz[:, D_RWKV:2 * D_RWKV]
    v = zz[:, 2 * D_RWKV:3 * D_RWKV]
    o = 3 * D_RWKV
    w_lo = zz[:, o:o + LORA_W]
    a_lo = zz[:, o + LORA_W:o + LORA_W + LORA_A]
    g_lo = zz[:, o + LORA_W + LORA_A:o + LORA_W + LORA_A + LORA_G]

    wl = w0_ref[...] + _dot(jnp.tanh(w_lo).astype(BF16), w2_ref[...])
    sp = jnp.maximum(-wl, 0.0) + jnp.log(1.0 + jnp.exp(-jnp.abs(wl)))
    ld_out[...] = -jnp.exp(-sp - 0.5)
    a = _sigmoid(a0_ref[...] + _dot(a_lo.astype(BF16), a2_ref[...]))
    g_out[...] = _dot(_sigmoid(g_lo).astype(BF16), g2_ref[...])

    kk = k * kk_ref[...]
    n2 = _dot_split_lhs(kk * kk, bd_ref[...], 2)
    kk = kk / jnp.maximum(jnp.sqrt(n2), 1e-12)
    r_out[...] = r
    k_out[...] = k * (1.0 + (a - 1.0) * ka_ref[...])
    v_out[...] = v
    a_out[...] = -kk
    b_out[...] = kk * a


def _head_block_diag(n, dtype, value=1.0):
    i = jnp.arange(n) // HEAD
    return jnp.where(i[:, None] == i[None, :], value, 0.0).astype(dtype)


def _inproj(x, sh_m, sc_m, w_in, mu, w0, w2, a0, a2, g2, k_k, k_a, tm):
    bsz, seq, d = x.shape
    row = lambda a: a.reshape(1, -1)
    const = lambda shape: pl.BlockSpec(shape, lambda b, t: (0,) * len(shape))
    tok = lambda n: pl.BlockSpec((None, tm, n), lambda b, t: (b, t, 0))
    per_b = pl.BlockSpec((None, 1, d), lambda b, t: (b, 0, 0))
    out_sds = jax.ShapeDtypeStruct((bsz, seq, D_RWKV), F32)
    return pl.pallas_call(
        _inproj_kernel,
        grid=(bsz, seq // tm),
        in_specs=[tok(d), per_b, per_b, const((d, D_IN)), const((1, D_SHIFT)),
                  const((1, D_RWKV)), const((LORA_W, D_RWKV)), const((1, D_RWKV)),
                  const((LORA_A, D_RWKV)), const((LORA_G, D_RWKV)), const((1, D_RWKV)),
                  const((1, D_RWKV)), const((D_RWKV, D_RWKV))],
        out_specs=[tok(D_RWKV)] * 8,
        out_shape=[out_sds] * 8,
        scratch_shapes=[pltpu.VMEM((1, D_SHIFT), F32)],
        compiler_params=pltpu.CompilerParams(dimension_semantics=("parallel", "arbitrary"),
                                             vmem_limit_bytes=VMEM_LIMIT),
    )(x, sh_m.reshape(bsz, 1, d), sc_m.reshape(bsz, 1, d), w_in.astype(BF16), row(mu),
      row(w0), w2.astype(BF16), row(a0), a2.astype(BF16), g2.astype(BF16), row(k_k), row(k_a),
      _head_block_diag(D_RWKV, BF16))


def _rwkv_chunk_maps(slabs, cst):
    tri, strict, incl, bd, eye, m0, m1, m0w, m1w, eye_c, merge_masks = cst
    c = CHUNK
    r, k, v, ld, a, b = ([s[i] for s in slabs] for i in range(6))
    parts = [_split_bf16(x, 3) for x in ld]
    cum = [_dot(tri, p[0]) + _dot(tri, p[1]) + _dot(tri, p[2]) for p in parts]
    cl = [x[c - 1:c, :] for x in cum]
    w_to = [jnp.exp(x) for x in cum]
    w_inv = [jnp.exp(-x) for x in cum]
    w_prev = [jnp.exp(x - y) for x, y in zip(cum, ld)]
    w_rem = [jnp.exp(y - x) for x, y in zip(cum, cl)]
    w_all = [jnp.exp(y) for y in cl]
    rt = [x * w for x, w in zip(r, w_to)]
    kt = [x * w for x, w in zip(k, w_inv)]
    at = [x * w for x, w in zip(a, w_prev)]
    bt = [x * w for x, w in zip(b, w_inv)]
    bh = [x * w for x, w in zip(b, w_rem)]
    kh = [x * w for x, w in zip(k, w_rem)]

    lhs = [jnp.concatenate([x, y], axis=0).astype(BF16) for x, y in zip(at, rt)]
    rhs = [jnp.concatenate([x * m0, x * m1, y * m0, y * m1], axis=0).astype(BF16) for x, y in zip(bt, kt)]
    a_all = [_dot_nt(x, y) for x, y in zip(lhs, rhs)]
    n_ab = [jnp.where(strict, x[:c, :PAIR], 0.0) for x in a_all]
    a_ak = [jnp.where(strict, x[:c, PAIR:], 0.0) for x in a_all]
    a_rb = [jnp.where(incl, x[c:, :PAIR], 0.0) for x in a_all]
    a_rk = [jnp.where(incl, x[c:, PAIR:], 0.0) for x in a_all]

    vs = [jnp.concatenate([x * m0, x * m1], axis=0).astype(BF16) for x in v]
    x0 = [jnp.concatenate([y, _dot(z.astype(BF16), w)], axis=1) for y, z, w in zip(at, a_ak, vs)]

    def stack(y):
        return jnp.concatenate([y * m0w, y * m1w], axis=0).astype(BF16)

    def block_diag(y):
        return (jnp.concatenate([y, y], axis=0) * bd).astype(BF16)

    t_inv = [eye_c] * len(slabs)
    for msk in merge_masks:
        n_off = [jnp.where(msk, x, 0.0) for x in n_ab]
        tmp = [_dot(t.astype(BF16), block_diag(o)) for t, o in zip(t_inv, n_off)]
        t_inv = [t + _dot(m.astype(BF16), block_diag(t)) for t, m in zip(t_inv, tmp)]
    x1 = [_dot(t.astype(BF16), stack(y)) for t, y in zip(t_inv, x0)]
    rx = [_dot(z.astype(BF16), stack(y)) for z, y in zip(a_rb, x1)]
    r_hat = [x + y[:, :PAIR] for x, y in zip(rt, rx)]
    y_hat = [y[:, PAIR:] + _dot(z.astype(BF16), w) for y, z, w in zip(rx, a_rk, vs)]
    g1 = [_dot(x.T.astype(BF16), y.astype(BF16)) for x, y in zip(bh, x1)]
    g2 = [_dot(x.T.astype(BF16), y.astype(BF16)) for x, y in zip(kh, v)]
    m_mat = [eye * w + x[:, :PAIR] * bd for w, x in zip(w_all, g1)]
    s_hat = [(x[:, PAIR:] + y) * bd for x, y in zip(g1, g2)]
    return list(zip(r_hat, y_hat, m_mat, s_hat))


def _rwkv_kernel(r_ref, k_ref, v_ref, ld_ref, a_ref, b_ref, g_ref, rk_ref, lw_ref, lb_ref,
                 tri_ref, bd_ref, avg_ref, y_ref, s_ref):
    t = pl.program_id(1)

    @pl.when(t == 0)
    def _():
        s_ref[...] = jnp.zeros_like(s_ref)

    c = CHUNK
    lane = lax.broadcasted_iota(jnp.int32, (1, PAIR), 1)
    m0 = (lane < HEAD).astype(F32)
    m1 = 1.0 - m0
    lane_w = lax.broadcasted_iota(jnp.int32, (1, 2 * PAIR), 1) % PAIR
    m0w = (lane_w < HEAD).astype(F32)
    m1w = 1.0 - m0w
    row = lax.broadcasted_iota(jnp.int32, (c, PAIR), 0)
    col = lax.broadcasted_iota(jnp.int32, (c, PAIR), 1) % c
    strict = col < row
    incl = col <= row
    bd = bd_ref[...]
    ri = lax.broadcasted_iota(jnp.int32, (PAIR, PAIR), 0)
    ci = lax.broadcasted_iota(jnp.int32, (PAIR, PAIR), 1)
    eye = (ri == ci).astype(F32)
    eye_c = (col == row).astype(F32)
    merge_masks = []
    sz = 1
    while sz < c:
        merge_masks.append((row // (2 * sz) == col // (2 * sz)) & (row % (2 * sz) >= sz) & (col % (2 * sz) < sz))
        sz *= 2
    cst = (tri_ref[...], strict, incl, bd, eye, m0, m1, m0w, m1w, eye_c, merge_masks)
    avg = avg_ref[...]
    ones_bd = bd.astype(BF16)

    n_pairs = D_RWKV // PAIR
    span = UNIT_CHUNKS * c
    units = [(ch, p) for ch in range(UNIT_CHUNKS) for p in range(n_pairs)]

    def body(i, carry):
        base = pl.multiple_of(i * span, span)
        rows = [pl.ds(base + ch * c, c) for ch, _ in units]
        lanes = [slice(p * PAIR, (p + 1) * PAIR) for _, p in units]
        slabs = [tuple(ref[rs, ls] for ref in (r_ref, k_ref, v_ref, ld_ref, a_ref, b_ref))
                 for rs, ls in zip(rows, lanes)]
        maps = _rwkv_chunk_maps(slabs, cst)

        state = [s_ref[p] for p in range(n_pairs)]
        ys = []
        for (ch, p), (r_hat, y_hat, m_mat, s_hat) in zip(units, maps):
            sb = state[p].astype(BF16)
            ym = _dot(jnp.concatenate([r_hat, m_mat], axis=0).astype(BF16), sb)
            ys.append(ym[:c] + y_hat)
            state[p] = ym[c:] + s_hat
        for p in range(n_pairs):
            s_ref[p] = state[p]

        mu = [_dot_split_lhs(y, avg, 2) for y in ys]
        dlt = [y - m for y, m in zip(ys, mu)]
        var = [_dot_split_lhs(x * x, avg, 2) for x in dlt]
        bonus = [_dot_split_lhs(s[0] * s[1] * rk_ref[:, ls], ones_bd, 2) for s, ls in zip(slabs, lanes)]
        for x, vr, bn, s, rs, ls in zip(dlt, var, bonus, slabs, rows, lanes):
            yn = x * lax.rsqrt(vr + LNX_EPS) * lw_ref[:, ls] + lb_ref[:, ls]
            y_ref[rs, ls] = (yn + bn * s[2]) * g_ref[rs, ls]
        return carry

    lax.fori_loop(0, r_ref.shape[0] // span, body, 0)


def _rwkv(r, k, v, ld, a, b, g, r_k, lnx_w, lnx_b, tb):
    bsz, seq, d = r.shape
    tok = pl.BlockSpec((None, tb, d), lambda i, t: (i, t, 0))
    const = lambda shape: pl.BlockSpec(shape, lambda i, t: (0,) * len(shape))
    tri = jnp.tril(jnp.ones((CHUNK, CHUNK), F32)).astype(BF16)
    return pl.pallas_call(
        _rwkv_kernel,
        grid=(bsz, seq // tb),
        in_specs=[tok] * 7 + [const((1, d))] * 3 + [const((CHUNK, CHUNK)), const((PAIR, PAIR)), const((PAIR, PAIR))],
        out_specs=tok,
        out_shape=jax.ShapeDtypeStruct((bsz, seq, d), F32),
        scratch_shapes=[pltpu.VMEM((d // PAIR, PAIR, PAIR), F32)],
        compiler_params=pltpu.CompilerParams(dimension_semantics=("parallel", "arbitrary"),
                                             vmem_limit_bytes=VMEM_LIMIT),
    )(r, k, v, ld, a, b, g, r_k.reshape(1, d), lnx_w.reshape(1, d), lnx_b.reshape(1, d),
      tri, _head_block_diag(PAIR, F32), _head_block_diag(PAIR, BF16, 1.0 / HEAD))


def _s5_prep_kernel(are_ref, aim_ref, ldt_ref, btre_ref, btim_ref, cre_ref, cim_ref,
                    x_out, p_out, k_out, lam_out):
    a_re = are_ref[...]
    a_im = aim_ref[...]
    dt = jnp.exp(ldt_ref[...])
    nd = 24
    dpow = lax.broadcasted_iota(jnp.int32, (nd, S5_STATE), 0).astype(F32)
    mag = jnp.exp(dpow * (dt * a_re))
    ang = dpow * (dt * a_im)
    e_re = mag * jnp.cos(ang)
    e_im = mag * jnp.sin(ang)
    lam_re = e_re[1:2, :]
    lam_im = e_im[1:2, :]
    den = a_re * a_re + a_im * a_im
    pp = lam_re - 1.0
    qq = lam_im
    coef_re = (pp * a_re + qq * a_im) / den
    coef_im = (qq * a_re - pp * a_im) / den
    bt_re = btre_ref[...]
    bt_im = btim_ref[...]
    bb_re = coef_re * bt_re - coef_im * bt_im
    bb_im = coef_re * bt_im + coef_im * bt_re
    c_re = cre_ref[...]
    c_im = cim_ref[...]
    for d in range(S5_L + 1):
        er = e_re[d:d + 1, :]
        ei = e_im[d:d + 1, :]
        x_out[d * S5_CH:(d + 1) * S5_CH, :] = jnp.concatenate(
            [c_re * er - c_im * ei, -(c_re * ei + c_im * er)], axis=1)
    for j in range(S5_L):
        d = S5_L - 1 - j
        er = e_re[d:d + 1, :]
        ei = e_im[d:d + 1, :]
        p_out[j * S5_CH:(j + 1) * S5_CH, :] = jnp.concatenate(
            [er * bb_re - ei * bb_im, er * bb_im + ei * bb_re], axis=1)
    y0 = jnp.concatenate([bb_re, bb_im], axis=1)
    k_out[...] = lax.dot_general(x_out[...], y0, (((1,), (1,)), ((), ())),
                                 preferred_element_type=F32, precision=lax.Precision.HIGHEST)
    lam_out[...] = jnp.concatenate([e_re[S5_L:S5_L + 1, :], e_im[S5_L:S5_L + 1, :]], axis=1)


def _s5_prep(a_re, a_im, log_dt, b_re, b_im, c_re, c_im):
    g, p = a_re.shape
    per_g = lambda shape: pl.BlockSpec((None,) + shape, lambda i: (i,) + (0,) * len(shape))
    nx = (S5_L + 1) * S5_CH
    return pl.pallas_call(
        _s5_prep_kernel,
        grid=(g,),
        in_specs=[per_g((1, p)), per_g((1, p)), per_g((1, 1)), per_g((S5_CH, p)), per_g((S5_CH, p)),
                  per_g((S5_CH, p)), per_g((S5_CH, p))],
        out_specs=[per_g((nx, 2 * p)), per_g((S5_L * S5_CH, 2 * p)), per_g((nx, S5_CH)), per_g((1, 2 * p))],
        out_shape=[jax.ShapeDtypeStruct((g, nx, 2 * p), F32),
                   jax.ShapeDtypeStruct((g, S5_L * S5_CH, 2 * p), F32),
                   jax.ShapeDtypeStruct((g, nx, S5_CH), F32),
                   jax.ShapeDtypeStruct((g, 1, 2 * p), F32)],
        compiler_params=pltpu.CompilerParams(dimension_semantics=("arbitrary",)),
    )(a_re.reshape(g, 1, p), a_im.reshape(g, 1, p), log_dt.reshape(g, 1, 1),
      jnp.swapaxes(b_re, 1, 2), jnp.swapaxes(b_im, 1, 2), c_re, c_im)


def _s5_local_kernel(u_ref, p_ref, z_ref):
    z_ref[...] = _dot(u_ref[...].astype(BF16), p_ref[...])


def _s5_scan_kernel(z_ref, lr_ref, li_ref, s_out, s_ref):
    @pl.when(pl.program_id(0) == 0)
    def _():
        s_ref[...] = jnp.zeros_like(s_ref)

    lr = lr_ref[...]
    li = li_ref[...]

    def body(i, s):
        s_out[i] = s
        return lr * s + li * pltpu.roll(s, S5_STATE, axis=1) + z_ref[i]

    s_ref[...] = lax.fori_loop(0, z_ref.shape[0], body, s_ref[...])


def _s5_out_kernel(u_ref, s_ref, toep_ref, qt_ref, dt_ref, y_ref):
    u = u_ref[...]
    y = _dot(u.astype(BF16), toep_ref[...])
    y = y + _dot_nt(s_ref[...].astype(BF16), qt_ref[...])
    y_ref[...] = y + u * dt_ref[...]


def _s5_core(u, a_re, a_im, log_dt, b_re, b_im, c_re, c_im, d_skip, scan_tile):
    bsz, seq, _ = u.shape
    g, ch, p, ll = S5_GROUPS, S5_CH, S5_STATE, S5_L
    nck = seq // ll
    n = nck * bsz
    w = ll * ch
    x_all, p_all, k_all, lam = _s5_prep(a_re, a_im, log_dt, b_re, b_im, c_re, c_im)

    qt = x_all[:, ch:, :].astype(BF16)
    p_mat = p_all.astype(BF16)
    kd = k_all.reshape(g, ll + 1, ch, ch)
    jj = jnp.arange(ll)
    lag = jj[None, :] - jj[:, None]
    toep = jnp.where((lag >= 0)[None, :, :, None, None], kd[:, jnp.clip(lag, 0, ll)], 0.0)
    toep = jnp.transpose(toep, (0, 1, 4, 2, 3)).reshape(g, w, w).astype(BF16)
    lam_re = jnp.broadcast_to(lam[:, :, :p], (g, bsz, p))
    lam_im = jnp.broadcast_to(lam[:, :, p:], (g, bsz, p))
    lr = jnp.concatenate([lam_re, lam_re], axis=-1).reshape(g * bsz, 2 * p)
    li = jnp.concatenate([-lam_im, lam_im], axis=-1).reshape(g * bsz, 2 * p)
    d_tile = jnp.tile(d_skip, (1, ll)).reshape(g, 1, w)

    ur = u.reshape(bsz, nck, ll, g, ch).transpose(3, 1, 0, 2, 4).reshape(g, n, w)

    per_g = lambda shape: pl.BlockSpec((None,) + shape, lambda i: (i,) + (0,) * len(shape))
    z = pl.pallas_call(
        _s5_local_kernel,
        grid=(g,),
        in_specs=[per_g((n, w)), per_g((w, 2 * p))],
        out_specs=per_g((n, 2 * p)),
        out_shape=jax.ShapeDtypeStruct((g, n, 2 * p), F32),
        compiler_params=pltpu.CompilerParams(dimension_semantics=("parallel",)),
    )(ur, p_mat)

    zs = z.reshape(g, nck, bsz, 2 * p).transpose(1, 0, 2, 3).reshape(nck, g * bsz, 2 * p)
    s_start = pl.pallas_call(
        _s5_scan_kernel,
        grid=(nck // scan_tile,),
        in_specs=[pl.BlockSpec((scan_tile, g * bsz, 2 * p), lambda i: (i, 0, 0)),
                  pl.BlockSpec((g * bsz, 2 * p), lambda i: (0, 0)),
                  pl.BlockSpec((g * bsz, 2 * p), lambda i: (0, 0))],
        out_specs=pl.BlockSpec((scan_tile, g * bsz, 2 * p), lambda i: (i, 0, 0)),
        out_shape=jax.ShapeDtypeStruct((nck, g * bsz, 2 * p), F32),
        scratch_shapes=[pltpu.VMEM((g * bsz, 2 * p), F32)],
        compiler_params=pltpu.CompilerParams(dimension_semantics=("arbitrary",)),
    )(zs, lr, li)
    sr = s_start.reshape(nck, g, bsz, 2 * p).transpose(1, 0, 2, 3).reshape(g, n, 2 * p)

    yr = pl.pallas_call(
        _s5_out_kernel,
        grid=(g,),
        in_specs=[per_g((n, w)), per_g((n, 2 * p)), per_g((w, w)), per_g((w, 2 * p)), per_g((1, w))],
        out_specs=per_g((n, w)),
        out_shape=jax.ShapeDtypeStruct((g, n, w), F32),
        compiler_params=pltpu.CompilerParams(dimension_semantics=("parallel",)),
    )(ur, sr, toep, qt, d_tile)
    return yr.reshape(g, nck, bsz, ll, ch).transpose(2, 1, 3, 0, 4).reshape(bsz, seq, g * ch)


def _tail_kernel(x_ref, yr_ref, ys_ref, gm_ref, shf_ref, scf_ref, gf_ref, wglu_ref, bglu_ref, gain_ref,
                 wo_ref, wg_ref, wu_ref, wd_ref, fg_ref, o_ref, *, ff_tile):
    ys = ys_ref[...]
    zz = 0.5 * ys * (1.0 + jnp.tanh(math.sqrt(2.0 / math.pi) * (ys + 0.044715 * (ys * ys * ys))))
    gl = zz * _sigmoid(_dot(zz.astype(BF16), wglu_ref[...]) + bglu_ref[...])
    gl = gl * lax.rsqrt(jnp.mean(gl * gl, axis=-1, keepdims=True) + NORM_EPS) * gain_ref[...]
    mix = _dot(yr_ref[...].astype(BF16), wo_ref[:D_RWKV, :]) + _dot(gl.astype(BF16), wo_ref[D_RWKV:, :])
    x1 = x_ref[...] + gm_ref[...] * mix
    h = x1 * lax.rsqrt(jnp.mean(x1 * x1, axis=-1, keepdims=True) + NORM_EPS)
    h = (h * (1.0 + scf_ref[...]) + shf_ref[...]).astype(BF16)
    acc = jnp.zeros_like(x1)
    for j in range(D_FF // ff_tile):
        cs = slice(j * ff_tile, (j + 1) * ff_tile)
        gate = _dot(h, wg_ref[:, cs])
        up = _dot(h, wu_ref[:, cs])
        act = gate * _sigmoid(gate) * up
        acc = acc + _dot(act.astype(BF16), wd_ref[cs, :])
    x2 = x1 + gf_ref[...] * acc
    o_ref[...] = x2 * lax.rsqrt(jnp.mean(x2 * x2, axis=-1, keepdims=True) + NORM_EPS) * fg_ref[...]


def _tail(x, y_rwkv, y_s5, g_m, sh_f, sc_f, g_f, w_glu, b_glu, gain, w_out, w_gate, w_up, w_down,
          final_gain, tm, ff_tile):
    bsz, seq, d = x.shape
    once = dict(pipeline_mode=pl.Buffered(1))
    const = lambda shape: pl.BlockSpec(shape, lambda b, t: (0,) * len(shape), **once)
    tok = lambda n: pl.BlockSpec((None, tm, n), lambda b, t: (b, t, 0))
    per_b = pl.BlockSpec((None, 1, d), lambda b, t: (b, 0, 0))
    b3 = lambda a: a.reshape(bsz, 1, d)
    return pl.pallas_call(
        functools.partial(_tail_kernel, ff_tile=ff_tile),
        grid=(bsz, seq // tm),
        in_specs=[tok(d), tok(D_RWKV), tok(D_S5), per_b, per_b, per_b, per_b,
                  const((D_S5, D_S5)), const((1, D_S5)), const((1, D_S5)),
                  const((D_RWKV + D_S5, d)), const((d, D_FF)), const((d, D_FF)), const((D_FF, d)),
                  const((1, d))],
        out_specs=tok(d),
        out_shape=jax.ShapeDtypeStruct((bsz, seq, d), F32),
        compiler_params=pltpu.CompilerParams(dimension_semantics=("parallel", "parallel"),
                                             vmem_limit_bytes=VMEM_LIMIT),
    )(x, y_rwkv, y_s5, b3(g_m), b3(sh_f), b3(sc_f), b3(g_f), w_glu.astype(BF16), b_glu.reshape(1, -1),
      gain.reshape(1, -1), w_out.astype(BF16), w_gate.astype(BF16), w_up.astype(BF16),
      w_down.astype(BF16), final_gain.reshape(1, d))


def kernel(x, c, w_ada, b_ada, w_in, mu_shift, rw_w0, rw_w2, rw_a0, rw_a2, rw_g2, rw_k_k, rw_k_a, rw_r_k,
           rw_lnx_w, rw_lnx_b, s5_a_re, s5_a_im, s5_log_dt, s5_b_re, s5_b_im, s5_c_re, s5_c_im, s5_d,
           s5_w_glu, s5_b_glu, s5_gain, w_out, ffn_w_gate, ffn_w_up, ffn_w_down, final_gain):
    assert w_ada.shape[0] == 1, "single-layer trunk"
    seq = x.shape[1]
    ada = _ada(c, w_ada[0], b_ada[0])
    sh_m, sc_m, g_m, sh_f, sc_f, g_f = jnp.split(ada, 6, axis=-1)
    r, k, v, ld, a, b, g, u = _inproj(x, sh_m, sc_m, w_in[0], mu_shift[0], rw_w0[0], rw_w2[0], rw_a0[0],
                                      rw_a2[0], rw_g2[0], rw_k_k[0], rw_k_a[0], tm=min(256, seq))
    y_rwkv = _rwkv(r, k, v, ld, a, b, g, rw_r_k[0], rw_lnx_w[0], rw_lnx_b[0], tb=min(512, seq))
    y_s5 = _s5_core(u, s5_a_re[0], s5_a_im[0], s5_log_dt[0], s5_b_re[0], s5_b_im[0], s5_c_re[0],
                    s5_c_im[0], s5_d[0], scan_tile=min(64, seq // S5_L))
    return _tail(x, y_rwkv, y_s5, g_m, sh_f, sc_f, g_f, s5_w_glu[0], s5_b_glu[0], s5_gain[0], w_out[0],
                 ffn_w_gate[0], ffn_w_up[0], ffn_w_down[0], final_gain, tm=min(512, seq), ff_tile=256)
```

```python
import functools
import math

import jax
import jax.numpy as jnp
from jax import lax
from jax.experimental import pallas as pl
from jax.experimental.pallas import tpu as pltpu

F32 = jnp.float32
BF16 = jnp.bfloat16

D_MODEL = 1024
D_RWKV = 512
D_S5 = 512
HEAD = 64
LORA_W = 64
LORA_A = 64
LORA_G = 128
S5_CH = 16
S5_GROUPS = 32
S5_STATE = 64
D_FF = 2816
D_SHIFT = 3 * D_RWKV + LORA_W + LORA_A + LORA_G
D_IN = D_SHIFT + D_S5
NORM_EPS = 1e-6
LNX_EPS = 64e-5

CHUNK = 64
PAIR = 2 * HEAD
UNIT_CHUNKS = 2
S5_L = 8
LANES = 128
S5_QG = LANES // S5_CH
S5_NQ = D_S5 // LANES
S5_W = S5_L * LANES
S5_SW = S5_QG * 2 * S5_STATE
S5_SROWS = S5_NQ * S5_SW // LANES
VMEM_LIMIT = 56 * 1024 * 1024


def _split_bf16(x, n):
    parts = []
    rem = x
    for i in range(n):
        p = rem.astype(BF16)
        parts.append(p)
        if i + 1 < n:
            rem = rem - p.astype(F32)
    return parts


def _dot(a, b):
    return jnp.dot(a, b, preferred_element_type=F32)


def _dot_nt(a, b):
    return lax.dot_general(a, b, (((1,), (1,)), ((), ())), preferred_element_type=F32)


def _dot_split_lhs(x, rhs_bf16, n):
    acc = None
    for p in _split_bf16(x, n):
        d = _dot(p, rhs_bf16)
        acc = d if acc is None else acc + d
    return acc


def _sigmoid(x):
    return 1.0 / (1.0 + jnp.exp(-x))


def _ada_kernel(c_ref, w_ref, b_ref, o_ref):
    c = c_ref[...]
    act = c * _sigmoid(c)
    o_ref[...] = jnp.dot(act, w_ref[...], preferred_element_type=F32,
                         precision=lax.Precision.HIGHEST) + b_ref[...]


def _ada(c, w_ada, b_ada):
    bsz, d = c.shape
    rows = 8
    c_pad = jnp.zeros((rows, d), F32).at[:bsz].set(c)
    n_out = w_ada.shape[1]
    out = pl.pallas_call(
        _ada_kernel,
        grid=(n_out // d,),
        in_specs=[
            pl.BlockSpec((rows, d), lambda j: (0, 0)),
            pl.BlockSpec((d, d), lambda j: (0, j)),
            pl.BlockSpec((1, d), lambda j: (0, j)),
        ],
        out_specs=pl.BlockSpec((rows, d), lambda j: (0, j)),
        out_shape=jax.ShapeDtypeStruct((rows, n_out), F32),
        compiler_params=pltpu.CompilerParams(dimension_semantics=("arbitrary",)),
    )(c_pad, w_ada, b_ada.reshape(1, n_out))
    return out[:bsz]


def _inproj_kernel(x_ref, sh_ref, sc_ref, win_ref, mu_ref, w0_ref, w2_ref, a0_ref, a2_ref, g2_ref,
                   kk_ref, ka_ref, bd_ref,
                   r_out, k_out, v_out, ld_out, a_out, b_out, g_out, u_out, carry_ref):
    t = pl.program_id(1)
    x = x_ref[...]
    tm = x.shape[0]
    ms = jnp.mean(x * x, axis=-1, keepdims=True)
    h = x * lax.rsqrt(ms + NORM_EPS)
    h = h * (1.0 + sc_ref[...]) + sh_ref[...]
    proj = _dot(h.astype(BF16), win_ref[...])
    z = proj[:, :D_SHIFT]
    u_out[...] = proj[:, D_SHIFT:].astype(u_out.dtype)

    @pl.when(t == 0)
    def _():
        carry_ref[...] = jnp.zeros_like(carry_ref)

    z_roll = pltpu.roll(z, 1, axis=0)
    row = lax.broadcasted_iota(jnp.int32, z.shape, 0)
    z_prev = jnp.where(row == 0, carry_ref[...], z_roll)
    carry_ref[...] = z[tm - 1:tm, :]
    zz = z + mu_ref[...] * (z_prev - z)

    r = zz[:, 0:D_RWKV]
    k = zz[:, D_RWKV:2 * D_RWKV]
    v = zz[:, 2 * D_RWKV:3 * D_RWKV]
    o = 3 * D_RWKV
    w_lo = zz[:, o:o + LORA_W]
    a_lo = zz[:, o + LORA_W:o + LORA_W + LORA_A]
    g_lo = zz[:, o + LORA_W + LORA_A:o + LORA_W + LORA_A + LORA_G]

    wl = w0_ref[...] + _dot(jnp.tanh(w_lo).astype(BF16), w2_ref[...])
    sp = jnp.maximum(-wl, 0.0) + jnp.log(1.0 + jnp.exp(-jnp.abs(wl)))
    ld_out[...] = -jnp.exp(-sp - 0.5)
    a = _sigmoid(a0_ref[...] + _dot(a_lo.astype(BF16), a2_ref[...]))
    g_out[...] = _dot(_sigmoid(g_lo).astype(BF16), g2_ref[...]).astype(g_out.dtype)

    kk = k * kk_ref[...]
    n2 = _dot_split_lhs(kk * kk, bd_ref[...], 2)
    kk = kk / jnp.maximum(jnp.sqrt(n2), 1e-12)
    r_out[...] = r.astype(r_out.dtype)
    k_out[...] = (k * (1.0 + (a - 1.0) * ka_ref[...])).astype(k_out.dtype)
    v_out[...] = v.astype(v_out.dtype)
    a_out[...] = (-kk).astype(a_out.dtype)
    b_out[...] = (kk * a).astype(b_out.dtype)


def _head_block_diag(n, dtype, value=1.0):
    i = jnp.arange(n) // HEAD
    return jnp.where(i[:, None] == i[None, :], value, 0.0).astype(dtype)


def _inproj(x, sh_m, sc_m, w_in, mu, w0, w2, a0, a2, g2, k_k, k_a, tm):
    bsz, seq, d = x.shape
    row = lambda a: a.reshape(1, -1)
    const = lambda shape: pl.BlockSpec(shape, lambda b, t: (0,) * len(shape))
    tok = lambda n: pl.BlockSpec((None, tm, n), lambda b, t: (b, t, 0))
    per_b = pl.BlockSpec((None, 1, d), lambda b, t: (b, 0, 0))
    out_sds = lambda dt: jax.ShapeDtypeStruct((bsz, seq, D_RWKV), dt)
    out_dtypes = [BF16, BF16, BF16, F32, BF16, BF16, BF16]
    return pl.pallas_call(
        _inproj_kernel,
        grid=(bsz, seq // tm),
        in_specs=[tok(d), per_b, per_b, const((d, D_IN)), const((1, D_SHIFT)),
                  const((1, D_RWKV)), const((LORA_W, D_RWKV)), const((1, D_RWKV)),
                  const((LORA_A, D_RWKV)), const((LORA_G, D_RWKV)), const((1, D_RWKV)),
                  const((1, D_RWKV)), const((D_RWKV, D_RWKV))],
        out_specs=[tok(D_RWKV)] * 8,
        out_shape=[out_sds(dt) for dt in out_dtypes] + [jax.ShapeDtypeStruct((bsz, seq, D_S5), BF16)],
        scratch_shapes=[pltpu.VMEM((1, D_SHIFT), F32)],
        compiler_params=pltpu.CompilerParams(dimension_semantics=("parallel", "arbitrary"),
                                             vmem_limit_bytes=VMEM_LIMIT),
    )(x, sh_m.reshape(bsz, 1, d), sc_m.reshape(bsz, 1, d), w_in.astype(BF16), row(mu),
      row(w0), w2.astype(BF16), row(a0), a2.astype(BF16), g2.astype(BF16), row(k_k), row(k_a),
      _head_block_diag(D_RWKV, BF16))


def _rwkv_chunk_maps(slabs, cst):
    tri, strict, incl, bd, eye, m0, m1, m0w, m1w, eye_c, merge_masks = cst
    c = CHUNK
    r, k, v, ld, a, b = ([s[i] for s in slabs] for i in range(6))
    parts = [_split_bf16(x, 3) for x in ld]
    cum = [_dot(tri, p[0]) + _dot(tri, p[1]) + _dot(tri, p[2]) for p in parts]
    cl = [x[c - 1:c, :] for x in cum]
    w_to = [jnp.exp(x) for x in cum]
    w_inv = [jnp.exp(-x) for x in cum]
    w_prev = [jnp.exp(x - y) for x, y in zip(cum, ld)]
    w_rem = [jnp.exp(y - x) for x, y in zip(cum, cl)]
    w_all = [jnp.exp(y) for y in cl]
    rt = [x * w for x, w in zip(r, w_to)]
    kt = [x * w for x, w in zip(k, w_inv)]
    at = [x * w for x, w in zip(a, w_prev)]
    bt = [x * w for x, w in zip(b, w_inv)]
    bh = [x * w for x, w in zip(b, w_rem)]
    kh = [x * w for x, w in zip(k, w_rem)]

    lhs = [jnp.concatenate([x, y], axis=0).astype(BF16) for x, y in zip(at, rt)]
    rhs = [jnp.concatenate([x * m0, x * m1, y * m0, y * m1], axis=0).astype(BF16) for x, y in zip(bt, kt)]
    a_all = [_dot_nt(x, y) for x, y in zip(lhs, rhs)]
    n_ab = [jnp.where(strict, x[:c, :PAIR], 0.0) for x in a_all]
    a_ak = [jnp.where(strict, x[:c, PAIR:], 0.0) for x in a_all]
    a_rb = [jnp.where(incl, x[c:, :PAIR], 0.0) for x in a_all]
    a_rk = [jnp.where(incl, x[c:, PAIR:], 0.0) for x in a_all]

    vs = [jnp.concatenate([x * m0, x * m1], axis=0).astype(BF16) for x in v]
    x0 = [jnp.concatenate([y, _dot(z.astype(BF16), w)], axis=1) for y, z, w in zip(at, a_ak, vs)]

    def stack(y):
        return jnp.concatenate([y * m0w, y * m1w], axis=0).astype(BF16)

    def block_diag(y):
        return (jnp.concatenate([y, y], axis=0) * bd).astype(BF16)

    t_inv = [eye_c] * len(slabs)
    for msk in merge_masks:
        n_off = [jnp.where(msk, x, 0.0) for x in n_ab]
        tmp = [_dot(t.astype(BF16), block_diag(o)) for t, o in zip(t_inv, n_off)]
        t_inv = [t + _dot(m.astype(BF16), block_diag(t)) for t, m in zip(t_inv, tmp)]
    x1 = [_dot(t.astype(BF16), stack(y)) for t, y in zip(t_inv, x0)]
    rx = [_dot(z.astype(BF16), stack(y)) for z, y in zip(a_rb, x1)]
    r_hat = [x + y[:, :PAIR] for x, y in zip(rt, rx)]
    y_hat = [y[:, PAIR:] + _dot(z.astype(BF16), w) for y, z, w in zip(rx, a_rk, vs)]
    g1 = [_dot(x.T.astype(BF16), y.astype(BF16)) for x, y in zip(bh, x1)]
    g2 = [_dot(x.T.astype(BF16), y.astype(BF16)) for x, y in zip(kh, v)]
    m_mat = [eye * w + x[:, :PAIR] * bd for w, x in zip(w_all, g1)]
    s_hat = [(x[:, PAIR:] + y) * bd for x, y in zip(g1, g2)]
    return list(zip(r_hat, y_hat, m_mat, s_hat))


def _rwkv_kernel(r_ref, k_ref, v_ref, ld_ref, a_ref, b_ref, g_ref, rk_ref, lw_ref, lb_ref,
                 tri_ref, bd_ref, avg_ref, y_ref, s_ref):
    t = pl.program_id(1)

    @pl.when(t == 0)
    def _():
        s_ref[...] = jnp.zeros_like(s_ref)

    c = CHUNK
    lane = lax.broadcasted_iota(jnp.int32, (1, PAIR), 1)
    m0 = (lane < HEAD).astype(F32)
    m1 = 1.0 - m0
    lane_w = lax.broadcasted_iota(jnp.int32, (1, 2 * PAIR), 1) % PAIR
    m0w = (lane_w < HEAD).astype(F32)
    m1w = 1.0 - m0w
    row = lax.broadcasted_iota(jnp.int32, (c, PAIR), 0)
    col = lax.broadcasted_iota(jnp.int32, (c, PAIR), 1) % c
    strict = col < row
    incl = col <= row
    bd = bd_ref[...]
    ri = lax.broadcasted_iota(jnp.int32, (PAIR, PAIR), 0)
    ci = lax.broadcasted_iota(jnp.int32, (PAIR, PAIR), 1)
    eye = (ri == ci).astype(F32)
    eye_c = (col == row).astype(F32)
    merge_masks = []
    sz = 1
    while sz < c:
        merge_masks.append((row // (2 * sz) == col // (2 * sz)) & (row % (2 * sz) >= sz) & (col % (2 * sz) < sz))
        sz *= 2
    cst = (tri_ref[...], strict, incl, bd, eye, m0, m1, m0w, m1w, eye_c, merge_masks)
    avg = avg_ref[...]
    ones_bd = bd.astype(BF16)

    n_pairs = D_RWKV // PAIR
    span = UNIT_CHUNKS * c
    units = [(ch, p) for ch in range(UNIT_CHUNKS) for p in range(n_pairs)]

    def body(i, carry):
        base = pl.multiple_of(i * span, span)
        rows = [pl.ds(base + ch * c, c) for ch, _ in units]
        lanes = [slice(p * PAIR, (p + 1) * PAIR) for _, p in units]
        slabs = [tuple(ref[rs, ls].astype(F32) for ref in (r_ref, k_ref, v_ref, ld_ref, a_ref, b_ref))
                 for rs, ls in zip(rows, lanes)]
        maps = _rwkv_chunk_maps(slabs, cst)

        state = [s_ref[p] for p in range(n_pairs)]
        ys = []
        for (ch, p), (r_hat, y_hat, m_mat, s_hat) in zip(units, maps):
            sb = state[p].astype(BF16)
            ym = _dot(jnp.concatenate([r_hat, m_mat], axis=0).astype(BF16), sb)
            ys.append(ym[:c] + y_hat)
            state[p] = ym[c:] + s_hat
        for p in range(n_pairs):
            s_ref[p] = state[p]

        mu = [_dot_split_lhs(y, avg, 2) for y in ys]
        dlt = [y - m for y, m in zip(ys, mu)]
        var = [_dot_split_lhs(x * x, avg, 2) for x in dlt]
        bonus = [_dot_split_lhs(s[0] * s[1] * rk_ref[:, ls], ones_bd, 2) for s, ls in zip(slabs, lanes)]
        for x, vr, bn, s, rs, ls in zip(dlt, var, bonus, slabs, rows, lanes):
            yn = x * lax.rsqrt(vr + LNX_EPS) * lw_ref[:, ls] + lb_ref[:, ls]
            y_ref[rs, ls] = ((yn + bn * s[2]) * g_ref[rs, ls].astype(F32)).astype(y_ref.dtype)
        return carry

    lax.fori_loop(0, r_ref.shape[0] // span, body, 0)


def _rwkv(r, k, v, ld, a, b, g, r_k, lnx_w, lnx_b, tb):
    bsz, seq, d = r.shape
    tok = pl.BlockSpec((None, tb, d), lambda i, t: (i, t, 0))
    const = lambda shape: pl.BlockSpec(shape, lambda i, t: (0,) * len(shape))
    tri = jnp.tril(jnp.ones((CHUNK, CHUNK), F32)).astype(BF16)
    return pl.pallas_call(
        _rwkv_kernel,
        grid=(bsz, seq // tb),
        in_specs=[tok] * 7 + [const((1, d))] * 3 + [const((CHUNK, CHUNK)), const((PAIR, PAIR)), const((PAIR, PAIR))],
        out_specs=tok,
        out_shape=jax.ShapeDtypeStruct((bsz, seq, d), BF16),
        scratch_shapes=[pltpu.VMEM((d // PAIR, PAIR, PAIR), F32)],
        compiler_params=pltpu.CompilerParams(dimension_semantics=("parallel", "arbitrary"),
                                             vmem_limit_bytes=VMEM_LIMIT),
    )(r, k, v, ld, a, b, g, r_k.reshape(1, d), lnx_w.reshape(1, d), lnx_b.reshape(1, d),
      tri, _head_block_diag(PAIR, F32), _head_block_diag(PAIR, BF16, 1.0 / HEAD))


def _s5_prep_kernel(are_ref, aim_ref, ldt_ref, btre_ref, btim_ref, cre_ref, cim_ref,
                    x_out, p_out, k_out, lam_out):
    a_re = are_ref[...]
    a_im = aim_ref[...]
    dt = jnp.exp(ldt_ref[...])
    nd = 24
    dpow = lax.broadcasted_iota(jnp.int32, (nd, S5_STATE), 0).astype(F32)
    mag = jnp.exp(dpow * (dt * a_re))
    ang = dpow * (dt * a_im)
    e_re = mag * jnp.cos(ang)
    e_im = mag * jnp.sin(ang)
    lam_re = e_re[1:2, :]
    lam_im = e_im[1:2, :]
    den = a_re * a_re + a_im * a_im
    pp = lam_re - 1.0
    qq = lam_im
    coef_re = (pp * a_re + qq * a_im) / den
    coef_im = (qq * a_re - pp * a_im) / den
    bt_re = btre_ref[...]
    bt_im = btim_ref[...]
    bb_re = coef_re * bt_re - coef_im * bt_im
    bb_im = coef_re * bt_im + coef_im * bt_re
    c_re = cre_ref[...]
    c_im = cim_ref[...]
    for d in range(S5_L + 1):
        er = e_re[d:d + 1, :]
        ei = e_im[d:d + 1, :]
        x_out[d * S5_CH:(d + 1) * S5_CH, :] = jnp.concatenate(
            [c_re * er - c_im * ei, -(c_re * ei + c_im * er)], axis=1)
    for j in range(S5_L):
        d = S5_L - 1 - j
        er = e_re[d:d + 1, :]
        ei = e_im[d:d + 1, :]
        p_out[j * S5_CH:(j + 1) * S5_CH, :] = jnp.concatenate(
            [er * bb_re - ei * bb_im, er * bb_im + ei * bb_re], axis=1)
    y0 = jnp.concatenate([bb_re, bb_im], axis=1)
    k_out[...] = lax.dot_general(x_out[...], y0, (((1,), (1,)), ((), ())),
                                 preferred_element_type=F32, precision=lax.Precision.HIGHEST)
    lam_out[...] = jnp.concatenate([e_re[S5_L:S5_L + 1, :], e_im[S5_L:S5_L + 1, :]], axis=1)


def _s5_prep(a_re, a_im, log_dt, b_re, b_im, c_re, c_im):
    g, p = a_re.shape
    per_g = lambda shape: pl.BlockSpec((None,) + shape, lambda i: (i,) + (0,) * len(shape))
    nx = (S5_L + 1) * S5_CH
    return pl.pallas_call(
        _s5_prep_kernel,
        grid=(g,),
        in_specs=[per_g((1, p)), per_g((1, p)), per_g((1, 1)), per_g((S5_CH, p)), per_g((S5_CH, p)),
                  per_g((S5_CH, p)), per_g((S5_CH, p))],
        out_specs=[per_g((nx, 2 * p)), per_g((S5_L * S5_CH, 2 * p)), per_g((nx, S5_CH)), per_g((1, 2 * p))],
        out_shape=[jax.ShapeDtypeStruct((g, nx, 2 * p), F32),
                   jax.ShapeDtypeStruct((g, S5_L * S5_CH, 2 * p), F32),
                   jax.ShapeDtypeStruct((g, nx, S5_CH), F32),
                   jax.ShapeDtypeStruct((g, 1, 2 * p), F32)],
        compiler_params=pltpu.CompilerParams(dimension_semantics=("arbitrary",)),
    )(a_re.reshape(g, 1, p), a_im.reshape(g, 1, p), log_dt.reshape(g, 1, 1),
      jnp.swapaxes(b_re, 1, 2), jnp.swapaxes(b_im, 1, 2), c_re, c_im)


def _s5_kernel(u_ref, bp_ref, bt_ref, bq_ref, lr_ref, li_ref, y_ref, z_ref, s_ref):
    @pl.when(pl.program_id(0) == 0)
    def _():
        s_ref[...] = jnp.zeros_like(s_ref)

    nb, ct, _ = u_ref.shape
    rows = nb * ct
    kb = S5_SW // LANES
    half_rows = S5_SROWS // 2

    def cols(q):
        return [slice(tl * D_S5 + q * LANES, tl * D_S5 + (q + 1) * LANES) for tl in range(S5_L)]

    def x_of(q):
        return jnp.concatenate([u_ref[:, :, cs].reshape(rows, LANES) for cs in cols(q)], axis=1)

    def srow(q, k):
        j = (k // (kb // 2)) * half_rows + q * (kb // 2) + k % (kb // 2)
        return pl.ds(j, rows, stride=S5_SROWS)

    for q in range(S5_NQ):
        z = _dot(x_of(q), bp_ref[q])
        for k in range(kb):
            z_ref[srow(q, k), :] = z[:, k * LANES:(k + 1) * LANES]

    lr = lr_ref[...]
    li = li_ref[...]

    def swap_re_im(s):
        pieces = []
        for b in range(nb):
            lo = b * S5_SROWS
            pieces += [s[lo + half_rows:lo + S5_SROWS], s[lo:lo + half_rows]]
        return jnp.concatenate(pieces, axis=0)

    def step(c, s):
        at = [pl.ds(pl.multiple_of((b * ct + c) * S5_SROWS, S5_SROWS), S5_SROWS) for b in range(nb)]
        z = jnp.concatenate([z_ref[at[b], :] for b in range(nb)], axis=0)
        for b in range(nb):
            z_ref[at[b], :] = s[b * S5_SROWS:(b + 1) * S5_SROWS]
        return lr * s + li * swap_re_im(s) + z

    s_ref[...] = lax.fori_loop(0, ct, step, s_ref[...])

    for q in range(S5_NQ):
        start = jnp.concatenate([z_ref[srow(q, k), :] for k in range(kb)], axis=1).astype(BF16)
        y = (_dot(x_of(q), bt_ref[q]) + _dot(start, bq_ref[q])).astype(y_ref.dtype)
        for tl, cs in enumerate(cols(q)):
            y_ref[:, :, cs] = y[:, tl * LANES:(tl + 1) * LANES].reshape(nb, ct, LANES)


def _s5_core(u, a_re, a_im, log_dt, b_re, b_im, c_re, c_im, d_skip, ct):
    bsz, seq, _ = u.shape
    g, ch, p, ll = S5_GROUPS, S5_CH, S5_STATE, S5_L
    nq, qg = S5_NQ, S5_QG
    nck = seq // ll
    x_all, p_all, k_all, lam = _s5_prep(a_re, a_im, log_dt, b_re, b_im, c_re, c_im)

    eye_q = jnp.eye(qg, dtype=F32)
    kd = k_all.reshape(g, ll + 1, ch, ch)
    kd = kd.at[:, 0].add(d_skip[:, :, None] * jnp.eye(ch, dtype=F32))
    jj = jnp.arange(ll)
    lag = jj[None, :] - jj[:, None]
    toep = jnp.where((lag >= 0)[None, :, :, None, None], kd[:, jnp.clip(lag, 0, ll)], 0.0)
    toep = toep.reshape(nq, qg, ll, ll, ch, ch)
    big_t = jnp.einsum("qgjtca,gh->qjgathc", toep, eye_q).reshape(nq, S5_W, S5_W).astype(BF16)
    p6 = p_all.reshape(nq, qg, ll, ch, 2, p)
    big_p = jnp.einsum("qgjarp,gh->qjgarhp", p6, eye_q).reshape(nq, S5_W, S5_SW).astype(BF16)
    q6 = x_all[:, ch:, :].reshape(nq, qg, ll, ch, 2, p)
    big_q = jnp.einsum("qgtcrp,gh->qrgpthc", q6, eye_q).reshape(nq, S5_SW, S5_W).astype(BF16)
    lam_re = lam[:, 0, :p].reshape(S5_SROWS // 2, LANES)
    lam_im = lam[:, 0, p:].reshape(S5_SROWS // 2, LANES)
    lr = jnp.tile(jnp.concatenate([lam_re, lam_re], axis=0), (bsz, 1))
    li = jnp.tile(jnp.concatenate([-lam_im, lam_im], axis=0), (bsz, 1))

    u2 = u.reshape(bsz, nck, ll * D_S5)
    once = dict(pipeline_mode=pl.Buffered(1))
    const = lambda shape: pl.BlockSpec(shape, lambda i: (0,) * len(shape), **once)
    tok = pl.BlockSpec((bsz, ct, ll * D_S5), lambda i: (0, i, 0))
    y2 = pl.pallas_call(
        _s5_kernel,
        grid=(nck // ct,),
        in_specs=[tok, const((nq, S5_W, S5_SW)), const((nq, S5_W, S5_W)), const((nq, S5_SW, S5_W)),
                  const((bsz * S5_SROWS, LANES)), const((bsz * S5_SROWS, LANES))],
        out_specs=tok,
        out_shape=jax.ShapeDtypeStruct((bsz, nck, ll * D_S5), BF16),
        scratch_shapes=[pltpu.VMEM((bsz * ct * S5_SROWS, LANES), F32),
                        pltpu.VMEM((bsz * S5_SROWS, LANES), F32)],
        compiler_params=pltpu.CompilerParams(dimension_semantics=("arbitrary",),
                                             vmem_limit_bytes=VMEM_LIMIT),
    )(u2, big_p, big_t, big_q, lr, li)
    return y2.reshape(bsz, seq, D_S5)


def _tail_kernel(x_ref, yr_ref, ys_ref, gm_ref, shf_ref, scf_ref, gf_ref, wglu_ref, bglu_ref, gain_ref,
                 wo_ref, wg_ref, wu_ref, wd_ref, fg_ref, o_ref, *, ff_tile):
    ys = ys_ref[...].astype(F32)
    zz = 0.5 * ys * (1.0 + jnp.tanh(math.sqrt(2.0 / math.pi) * (ys + 0.044715 * (ys * ys * ys))))
    gl = zz * _sigmoid(_dot(zz.astype(BF16), wglu_ref[...]) + bglu_ref[...])
    gl = gl * lax.rsqrt(jnp.mean(gl * gl, axis=-1, keepdims=True) + NORM_EPS) * gain_ref[...]
    mix = _dot(yr_ref[...].astype(BF16), wo_ref[:D_RWKV, :]) + _dot(gl.astype(BF16), wo_ref[D_RWKV:, :])
    x1 = x_ref[...] + gm_ref[...] * mix
    h = x1 * lax.rsqrt(jnp.mean(x1 * x1, axis=-1, keepdims=True) + NORM_EPS)
    h = (h * (1.0 + scf_ref[...]) + shf_ref[...]).astype(BF16)
    acc = jnp.zeros_like(x1)
    for j in range(D_FF // ff_tile):
        cs = slice(j * ff_tile, (j + 1) * ff_tile)
        gate = _dot(h, wg_ref[:, cs])
        up = _dot(h, wu_ref[:, cs])
        act = gate * _sigmoid(gate) * up
        acc = acc + _dot(act.astype(BF16), wd_ref[cs, :])
    x2 = x1 + gf_ref[...] * acc
    o_ref[...] = x2 * lax.rsqrt(jnp.mean(x2 * x2, axis=-1, keepdims=True) + NORM_EPS) * fg_ref[...]


def _tail(x, y_rwkv, y_s5, g_m, sh_f, sc_f, g_f, w_glu, b_glu, gain, w_out, w_gate, w_up, w_down,
          final_gain, tm, ff_tile):
    bsz, seq, d = x.shape
    once = dict(pipeline_mode=pl.Buffered(1))
    const = lambda shape: pl.BlockSpec(shape, lambda b, t: (0,) * len(shape), **once)
    tok = lambda n: pl.BlockSpec((None, tm, n), lambda b, t: (b, t, 0))
    per_b = pl.BlockSpec((None, 1, d), lambda b, t: (b, 0, 0))
    b3 = lambda a: a.reshape(bsz, 1, d)
    return pl.pallas_call(
        functools.partial(_tail_kernel, ff_tile=ff_tile),
        grid=(bsz, seq // tm),
        in_specs=[tok(d), tok(D_RWKV), tok(D_S5), per_b, per_b, per_b, per_b,
                  const((D_S5, D_S5)), const((1, D_S5)), const((1, D_S5)),
                  const((D_RWKV + D_S5, d)), const((d, D_FF)), const((d, D_FF)), const((D_FF, d)),
                  const((1, d))],
        out_specs=tok(d),
        out_shape=jax.ShapeDtypeStruct((bsz, seq, d), F32),
        compiler_params=pltpu.CompilerParams(dimension_semantics=("parallel", "parallel"),
                                             vmem_limit_bytes=VMEM_LIMIT),
    )(x, y_rwkv, y_s5, b3(g_m), b3(sh_f), b3(sc_f), b3(g_f), w_glu.astype(BF16), b_glu.reshape(1, -1),
      gain.reshape(1, -1), w_out.astype(BF16), w_gate.astype(BF16), w_up.astype(BF16),
      w_down.astype(BF16), final_gain.reshape(1, d))


def kernel(x, c, w_ada, b_ada, w_in, mu_shift, rw_w0, rw_w2, rw_a0, rw_a2, rw_g2, rw_k_k, rw_k_a, rw_r_k,
           rw_lnx_w, rw_lnx_b, s5_a_re, s5_a_im, s5_log_dt, s5_b_re, s5_b_im, s5_c_re, s5_c_im, s5_d,
           s5_w_glu, s5_b_glu, s5_gain, w_out, ffn_w_gate, ffn_w_up, ffn_w_down, final_gain):
    assert w_ada.shape[0] == 1, "single-layer trunk"
    seq = x.shape[1]
    ada = _ada(c, w_ada[0], b_ada[0])
    sh_m, sc_m, g_m, sh_f, sc_f, g_f = jnp.split(ada, 6, axis=-1)
    r, k, v, ld, a, b, g, u = _inproj(x, sh_m, sc_m, w_in[0], mu_shift[0], rw_w0[0], rw_w2[0], rw_a0[0],
                                      rw_a2[0], rw_g2[0], rw_k_k[0], rw_k_a[0], tm=min(256, seq))
    y_rwkv = _rwkv(r, k, v, ld, a, b, g, rw_r_k[0], rw_lnx_w[0], rw_lnx_b[0], tb=min(512, seq))
    y_s5 = _s5_core(u, s5_a_re[0], s5_a_im[0], s5_log_dt[0], s5_b_re[0], s5_b_im[0], s5_c_re[0],
                    s5_c_im[0], s5_d[0], ct=min(128, seq // S5_L))
    return _tail(x, y_rwkv, y_s5, g_m, sh_f, sc_f, g_f, s5_w_glu[0], s5_b_glu[0], s5_gain[0], w_out[0],
                 ffn_w_gate[0], ffn_w_up[0], ffn_w_down[0], final_gain, tm=min(512, seq), ff_tile=256)
```

```python
import functools
import math

import jax
import jax.numpy as jnp
from jax import lax
from jax.experimental import pallas as pl
from jax.experimental.pallas import tpu as pltpu

F32 = jnp.float32
BF16 = jnp.bfloat16

D_MODEL = 1024
D_RWKV = 512
D_S5 = 512
HEAD = 64
LORA_W = 64
LORA_A = 64
LORA_G = 128
S5_CH = 16
S5_GROUPS = 32
S5_STATE = 64
D_FF = 2816
D_SHIFT = 3 * D_RWKV + LORA_W + LORA_A + LORA_G
D_IN = D_SHIFT + D_S5
NORM_EPS = 1e-6
LNX_EPS = 64e-5

CHUNK = 64
PAIR = 2 * HEAD
UNIT_CHUNKS = 4
S5_L = 8
LANES = 128
S5_QG = LANES // S5_CH
S5_NQ = D_S5 // LANES
S5_W = S5_L * LANES
S5_SW = S5_QG * 2 * S5_STATE
S5_SROWS = S5_NQ * S5_SW // LANES
VMEM_LIMIT = 56 * 1024 * 1024


def _split_bf16(x, n):
    parts = []
    rem = x
    for i in range(n):
        p = rem.astype(BF16)
        parts.append(p)
        if i + 1 < n:
            rem = rem - p.astype(F32)
    return parts


def _dot(a, b):
    return jnp.dot(a, b, preferred_element_type=F32)


def _dot_nt(a, b):
    return lax.dot_general(a, b, (((1,), (1,)), ((), ())), preferred_element_type=F32)


def _dot_split_lhs(x, rhs_bf16, n):
    acc = None
    for p in _split_bf16(x, n):
        d = _dot(p, rhs_bf16)
        acc = d if acc is None else acc + d
    return acc


def _sigmoid(x):
    return 1.0 / (1.0 + jnp.exp(-x))


def _ada_kernel(c_ref, w_ref, b_ref, o_ref):
    c = c_ref[...]
    act = c * _sigmoid(c)
    o_ref[...] = jnp.dot(act, w_ref[...], preferred_element_type=F32,
                         precision=lax.Precision.HIGHEST) + b_ref[...]


def _ada(c, w_ada, b_ada):
    bsz, d = c.shape
    rows = 8
    c_pad = jnp.zeros((rows, d), F32).at[:bsz].set(c)
    n_out = w_ada.shape[1]
    out = pl.pallas_call(
        _ada_kernel,
        grid=(n_out // d,),
        in_specs=[
            pl.BlockSpec((rows, d), lambda j: (0, 0)),
            pl.BlockSpec((d, d), lambda j: (0, j)),
            pl.BlockSpec((1, d), lambda j: (0, j)),
        ],
        out_specs=pl.BlockSpec((rows, d), lambda j: (0, j)),
        out_shape=jax.ShapeDtypeStruct((rows, n_out), F32),
        compiler_params=pltpu.CompilerParams(dimension_semantics=("arbitrary",)),
    )(c_pad, w_ada, b_ada.reshape(1, n_out))
    return out[:bsz]


def _inproj_kernel(x_ref, sh_ref, sc_ref, win_ref, mu_ref, w0_ref, w2_ref, a0_ref, a2_ref, g2_ref,
                   kk_ref, ka_ref, bd_ref,
                   r_out, k_out, v_out, ld_out, a_out, b_out, g_out, u_out, carry_ref, u_scr):
    t = pl.program_id(1)
    x = x_ref[...]
    tm = x.shape[0]
    ms = jnp.mean(x * x, axis=-1, keepdims=True)
    h = x * lax.rsqrt(ms + NORM_EPS)
    h = h * (1.0 + sc_ref[...]) + sh_ref[...]
    proj = _dot(h.astype(BF16), win_ref[...])
    z = proj[:, :D_SHIFT]
    for k in range(S5_NQ):
        u_scr[k] = proj[:, D_SHIFT + k * LANES:D_SHIFT + (k + 1) * LANES]
    for tl in range(S5_L):
        for k in range(S5_NQ):
            lo = tl * D_S5 + k * LANES
            u_out[:, lo:lo + LANES] = u_scr[k, pl.ds(tl, tm // S5_L, stride=S5_L), :].astype(u_out.dtype)

    @pl.when(t == 0)
    def _():
        carry_ref[...] = jnp.zeros_like(carry_ref)

    z_roll = pltpu.roll(z, 1, axis=0)
    row = lax.broadcasted_iota(jnp.int32, z.shape, 0)
    z_prev = jnp.where(row == 0, carry_ref[...], z_roll)
    carry_ref[...] = z[tm - 1:tm, :]
    zz = z + mu_ref[...] * (z_prev - z)

    r = zz[:, 0:D_RWKV]
    k = zz[:, D_RWKV:2 * D_RWKV]
    v = zz[:, 2 * D_RWKV:3 * D_RWKV]
    o = 3 * D_RWKV
    w_lo = zz[:, o:o + LORA_W]
    a_lo = zz[:, o + LORA_W:o + LORA_W + LORA_A]
    g_lo = zz[:, o + LORA_W + LORA_A:o + LORA_W + LORA_A + LORA_G]

    wl = w0_ref[...] + _dot(jnp.tanh(w_lo).astype(BF16), w2_ref[...])
    sp = jnp.maximum(-wl, 0.0) + jnp.log(1.0 + jnp.exp(-jnp.abs(wl)))
    ld_out[...] = -jnp.exp(-sp - 0.5)
    a = _sigmoid(a0_ref[...] + _dot(a_lo.astype(BF16), a2_ref[...]))
    g_out[...] = _dot(_sigmoid(g_lo).astype(BF16), g2_ref[...]).astype(g_out.dtype)

    kk = k * kk_ref[...]
    n2 = _dot_split_lhs(kk * kk, bd_ref[...], 2)
    kk = kk / jnp.maximum(jnp.sqrt(n2), 1e-12)
    r_out[...] = r.astype(r_out.dtype)
    k_out[...] = (k * (1.0 + (a - 1.0) * ka_ref[...])).astype(k_out.dtype)
    v_out[...] = v.astype(v_out.dtype)
    a_out[...] = (-kk).astype(a_out.dtype)
    b_out[...] = (kk * a).astype(b_out.dtype)


def _head_block_diag(n, dtype, value=1.0):
    i = jnp.arange(n) // HEAD
    return jnp.where(i[:, None] == i[None, :], value, 0.0).astype(dtype)


def _inproj(x, sh_m, sc_m, w_in, mu, w0, w2, a0, a2, g2, k_k, k_a, tm):
    bsz, seq, d = x.shape
    row = lambda a: a.reshape(1, -1)
    const = lambda shape: pl.BlockSpec(shape, lambda b, t: (0,) * len(shape))
    tok = lambda n: pl.BlockSpec((None, tm, n), lambda b, t: (b, t, 0))
    per_b = pl.BlockSpec((None, 1, d), lambda b, t: (b, 0, 0))
    out_sds = lambda dt: jax.ShapeDtypeStruct((bsz, seq, D_RWKV), dt)
    out_dtypes = [BF16, BF16, BF16, F32, BF16, BF16, BF16]
    return pl.pallas_call(
        _inproj_kernel,
        grid=(bsz, seq // tm),
        in_specs=[tok(d), per_b, per_b, const((d, D_IN)), const((1, D_SHIFT)),
                  const((1, D_RWKV)), const((LORA_W, D_RWKV)), const((1, D_RWKV)),
                  const((LORA_A, D_RWKV)), const((LORA_G, D_RWKV)), const((1, D_RWKV)),
                  const((1, D_RWKV)), const((D_RWKV, D_RWKV))],
        out_specs=[tok(D_RWKV)] * 7 + [pl.BlockSpec((None, tm // S5_L, S5_L * D_S5), lambda b, t: (b, t, 0))],
        out_shape=[out_sds(dt) for dt in out_dtypes]
        + [jax.ShapeDtypeStruct((bsz, seq // S5_L, S5_L * D_S5), BF16)],
        scratch_shapes=[pltpu.VMEM((1, D_SHIFT), F32), pltpu.VMEM((S5_NQ, tm, LANES), F32)],
        compiler_params=pltpu.CompilerParams(dimension_semantics=("parallel", "arbitrary"),
                                             vmem_limit_bytes=VMEM_LIMIT),
    )(x, sh_m.reshape(bsz, 1, d), sc_m.reshape(bsz, 1, d), w_in.astype(BF16), row(mu),
      row(w0), w2.astype(BF16), row(a0), a2.astype(BF16), g2.astype(BF16), row(k_k), row(k_a),
      _head_block_diag(D_RWKV, BF16))


def _rwkv_chunk_maps(slabs, cst):
    row, strict, incl, bd, eye, m0, m1, m0w, m1w, eye_c, merge_masks = cst
    c = CHUNK
    r, k, v, ld, a, b = ([s[i] for s in slabs] for i in range(6))
    cum = ld
    sft = 1
    while sft < c:
        if sft < 8:
            cum = [x + jnp.where(row >= sft, pltpu.roll(x, sft, axis=0), 0.0) for x in cum]
        else:
            pad = jnp.zeros((sft, PAIR), F32)
            cum = [x + jnp.concatenate([pad, x[:c - sft]], axis=0) for x in cum]
        sft *= 2
    cl = [x[c - 1:c, :] for x in cum]
    w_to = [jnp.exp(x) for x in cum]
    w_inv = [jnp.exp(-x) for x in cum]
    w_prev = [jnp.exp(x - y) for x, y in zip(cum, ld)]
    w_rem = [jnp.exp(y - x) for x, y in zip(cum, cl)]
    w_all = [jnp.exp(y) for y in cl]
    rt = [x * w for x, w in zip(r, w_to)]
    kt = [x * w for x, w in zip(k, w_inv)]
    at = [x * w for x, w in zip(a, w_prev)]
    bt = [x * w for x, w in zip(b, w_inv)]
    bh = [x * w for x, w in zip(b, w_rem)]
    kh = [x * w for x, w in zip(k, w_rem)]

    def per_head(yb, lo, hi):
        return jnp.concatenate([yb * lo, yb * hi], axis=0)

    lhs = [jnp.concatenate([x, y], axis=0).astype(BF16) for x, y in zip(at, rt)]
    rhs = [jnp.concatenate([per_head(x.astype(BF16), m0, m1), per_head(y.astype(BF16), m0, m1)], axis=0)
           for x, y in zip(bt, kt)]
    a_all = [_dot_nt(x, y) for x, y in zip(lhs, rhs)]
    n_ab = [jnp.where(strict, x[:c, :PAIR], 0.0) for x in a_all]
    a_ak = [jnp.where(strict, x[:c, PAIR:], 0.0) for x in a_all]
    a_rb = [jnp.where(incl, x[c:, :PAIR], 0.0) for x in a_all]
    a_rk = [jnp.where(incl, x[c:, PAIR:], 0.0) for x in a_all]

    vb = [x.astype(BF16) for x in v]
    vs = [per_head(x, m0, m1) for x in vb]
    x0 = [jnp.concatenate([y, _dot(z.astype(BF16), w)], axis=1) for y, z, w in zip(at, a_ak, vs)]

    nb = [x.astype(BF16) for x in n_ab]
    t_inv = [eye_c + x * merge_masks[0] for x in n_ab]
    for msk in merge_masks[1:]:
        tb = [t.astype(BF16) for t in t_inv]
        tmp = [_dot(t, per_head(n * msk, m0, m1)) for t, n in zip(tb, nb)]
        t_inv = [t + _dot(m.astype(BF16), per_head(h, m0, m1)) for t, m, h in zip(t_inv, tmp, tb)]
    x1 = [_dot(t.astype(BF16), per_head(y.astype(BF16), m0w, m1w)) for t, y in zip(t_inv, x0)]
    x1b = [x.astype(BF16) for x in x1]
    rx = [_dot(z.astype(BF16), per_head(y, m0w, m1w)) for z, y in zip(a_rb, x1b)]
    r_hat = [x + y[:, :PAIR] for x, y in zip(rt, rx)]
    y_hat = [y[:, PAIR:] + _dot(z.astype(BF16), w) for y, z, w in zip(rx, a_rk, vs)]
    g1 = [_dot(x.T.astype(BF16), y) for x, y in zip(bh, x1b)]
    g2 = [_dot(x.T.astype(BF16), y) for x, y in zip(kh, vb)]
    m_mat = [eye * w + x[:, :PAIR] * bd for w, x in zip(w_all, g1)]
    s_hat = [(x[:, PAIR:] + y) * bd for x, y in zip(g1, g2)]
    return list(zip(r_hat, y_hat, m_mat, s_hat))


def _rwkv_kernel(r_ref, k_ref, v_ref, ld_ref, a_ref, b_ref, g_ref, rk_ref, lw_ref, lb_ref,
                 bd_ref, avg_ref, y_ref, s_ref):
    t = pl.program_id(1)

    @pl.when(t == 0)
    def _():
        s_ref[...] = jnp.zeros_like(s_ref)

    c = CHUNK
    lane = lax.broadcasted_iota(jnp.int32, (1, PAIR), 1)
    m0 = (lane < HEAD).astype(F32).astype(BF16)
    m1 = (lane >= HEAD).astype(F32).astype(BF16)
    lane_w = lax.broadcasted_iota(jnp.int32, (1, 2 * PAIR), 1) % PAIR
    m0w = (lane_w < HEAD).astype(F32).astype(BF16)
    m1w = (lane_w >= HEAD).astype(F32).astype(BF16)
    row = lax.broadcasted_iota(jnp.int32, (c, PAIR), 0)
    col = lax.broadcasted_iota(jnp.int32, (c, PAIR), 1) % c
    strict = col < row
    incl = col <= row
    bd = bd_ref[...]
    ri = lax.broadcasted_iota(jnp.int32, (PAIR, PAIR), 0)
    ci = lax.broadcasted_iota(jnp.int32, (PAIR, PAIR), 1)
    eye = (ri == ci).astype(F32)
    eye_c = (col == row).astype(F32)
    merge_masks = []
    sz = 1
    while sz < c:
        msk = (row // (2 * sz) == col // (2 * sz)) & (row % (2 * sz) >= sz) & (col % (2 * sz) < sz)
        merge_masks.append(msk.astype(F32) if sz == 1 else msk.astype(F32).astype(BF16))
        sz *= 2
    cst = (row, strict, incl, bd, eye, m0, m1, m0w, m1w, eye_c, merge_masks)
    avg = avg_ref[...]
    ones_bd = bd.astype(BF16)

    n_pairs = D_RWKV // PAIR
    span = UNIT_CHUNKS * c
    units = [(ch, p) for ch in range(UNIT_CHUNKS) for p in range(n_pairs)]

    def body(i, carry):
        base = pl.multiple_of(i * span, span)
        rows = [pl.ds(base + ch * c, c) for ch, _ in units]
        lanes = [slice(p * PAIR, (p + 1) * PAIR) for _, p in units]
        slabs = [tuple(ref[rs, ls].astype(F32) for ref in (r_ref, k_ref, v_ref, ld_ref, a_ref, b_ref))
                 for rs, ls in zip(rows, lanes)]
        maps = _rwkv_chunk_maps(slabs, cst)

        state = [s_ref[p] for p in range(n_pairs)]
        ys = []
        for (ch, p), (r_hat, y_hat, m_mat, s_hat) in zip(units, maps):
            sb = state[p].astype(BF16)
            ym = _dot(jnp.concatenate([r_hat, m_mat], axis=0).astype(BF16), sb)
            ys.append(ym[:c] + y_hat)
            state[p] = ym[c:] + s_hat
        for p in range(n_pairs):
            s_ref[p] = state[p]

        y_all = jnp.concatenate(ys, axis=0)
        dlt = y_all - _dot_split_lhs(y_all, avg, 2)
        var = _dot_split_lhs(dlt * dlt, avg, 2)
        rk = jnp.concatenate([s[0] * s[1] * rk_ref[:, ls] for s, ls in zip(slabs, lanes)], axis=0)
        bonus = _dot_split_lhs(rk, ones_bd, 2)
        yn = dlt * lax.rsqrt(var + LNX_EPS)
        for n, (s, rs, ls) in enumerate(zip(slabs, rows, lanes)):
            un = slice(n * c, (n + 1) * c)
            y = yn[un] * lw_ref[:, ls] + lb_ref[:, ls] + bonus[un] * s[2]
            y_ref[rs, ls] = (y * g_ref[rs, ls].astype(F32)).astype(y_ref.dtype)
        return carry

    lax.fori_loop(0, r_ref.shape[0] // span, body, 0)


def _rwkv(r, k, v, ld, a, b, g, r_k, lnx_w, lnx_b, tb):
    bsz, seq, d = r.shape
    tok = pl.BlockSpec((None, tb, d), lambda i, t: (i, t, 0))
    const = lambda shape: pl.BlockSpec(shape, lambda i, t: (0,) * len(shape))
    return pl.pallas_call(
        _rwkv_kernel,
        grid=(bsz, seq // tb),
        in_specs=[tok] * 7 + [const((1, d))] * 3 + [const((PAIR, PAIR)), const((PAIR, PAIR))],
        out_specs=tok,
        out_shape=jax.ShapeDtypeStruct((bsz, seq, d), BF16),
        scratch_shapes=[pltpu.VMEM((d // PAIR, PAIR, PAIR), F32)],
        compiler_params=pltpu.CompilerParams(dimension_semantics=("parallel", "arbitrary"),
                                             vmem_limit_bytes=VMEM_LIMIT),
    )(r, k, v, ld, a, b, g, r_k.reshape(1, d), lnx_w.reshape(1, d), lnx_b.reshape(1, d),
      _head_block_diag(PAIR, F32), _head_block_diag(PAIR, BF16, 1.0 / HEAD))


def _s5_prep_kernel(are_ref, aim_ref, ldt_ref, btre_ref, btim_ref, cre_ref, cim_ref, dsk_ref,
                    qt_out, p_out, t_out, lam_out):
    a_re = are_ref[...]
    a_im = aim_ref[...]
    dt = jnp.exp(ldt_ref[...])
    nd = 24
    dpow = lax.broadcasted_iota(jnp.int32, (nd, S5_STATE), 0).astype(F32)
    mag = jnp.exp(dpow * (dt * a_re))
    ang = dpow * (dt * a_im)
    e_re = mag * jnp.cos(ang)
    e_im = mag * jnp.sin(ang)
    lam_re = e_re[1:2, :]
    lam_im = e_im[1:2, :]
    den = a_re * a_re + a_im * a_im
    pp = lam_re - 1.0
    qq = lam_im
    coef_re = (pp * a_re + qq * a_im) / den
    coef_im = (qq * a_re - pp * a_im) / den
    bt_re = btre_ref[...]
    bt_im = btim_ref[...]
    bb_re = coef_re * bt_re - coef_im * bt_im
    bb_im = coef_re * bt_im + coef_im * bt_re
    c_re = cre_ref[...]
    c_im = cim_ref[...]
    c_lam = []
    for d in range(S5_L + 1):
        er = e_re[d:d + 1, :]
        ei = e_im[d:d + 1, :]
        c_lam.append(jnp.concatenate([c_re * er - c_im * ei, -(c_re * ei + c_im * er)], axis=1))
    qt_out[...] = jnp.concatenate(c_lam[1:], axis=0)
    for j in range(S5_L):
        d = S5_L - 1 - j
        er = e_re[d:d + 1, :]
        ei = e_im[d:d + 1, :]
        p_out[j * S5_CH:(j + 1) * S5_CH, :] = jnp.concatenate(
            [er * bb_re - ei * bb_im, er * bb_im + ei * bb_re], axis=1)
    y0 = jnp.concatenate([bb_re, bb_im], axis=1)
    for j in range(S5_L):
        lagged = jnp.concatenate([jnp.zeros_like(c_lam[0])] * j + c_lam[:S5_L - j], axis=0)
        t_out[j * S5_CH:(j + 1) * S5_CH, :] = lax.dot_general(
            y0, lagged, (((1,), (1,)), ((), ())), preferred_element_type=F32, precision=lax.Precision.HIGHEST)
    gw = S5_L * S5_CH
    on_diag = lax.broadcasted_iota(jnp.int32, (gw, gw), 0) == lax.broadcasted_iota(jnp.int32, (gw, gw), 1)
    t_out[...] = t_out[...] + jnp.where(on_diag, dsk_ref[...], 0.0)
    lam_out[...] = jnp.concatenate([e_re[S5_L:S5_L + 1, :], e_im[S5_L:S5_L + 1, :]], axis=1)


def _s5_prep(a_re, a_im, log_dt, b_re, b_im, c_re, c_im, d_skip):
    g, p = a_re.shape
    per_g = lambda shape: pl.BlockSpec((None,) + shape, lambda i: (i,) + (0,) * len(shape))
    gw = S5_L * S5_CH
    return pl.pallas_call(
        _s5_prep_kernel,
        grid=(g,),
        in_specs=[per_g((1, p)), per_g((1, p)), per_g((1, 1)), per_g((S5_CH, p)), per_g((S5_CH, p)),
                  per_g((S5_CH, p)), per_g((S5_CH, p)), per_g((1, gw))],
        out_specs=[per_g((gw, 2 * p)), per_g((gw, 2 * p)), per_g((gw, gw)), per_g((1, 2 * p))],
        out_shape=[jax.ShapeDtypeStruct((g, gw, 2 * p), F32),
                   jax.ShapeDtypeStruct((g, gw, 2 * p), F32),
                   jax.ShapeDtypeStruct((g, gw, gw), F32),
                   jax.ShapeDtypeStruct((g, 1, 2 * p), F32)],
        compiler_params=pltpu.CompilerParams(dimension_semantics=("arbitrary",)),
    )(a_re.reshape(g, 1, p), a_im.reshape(g, 1, p), log_dt.reshape(g, 1, 1),
      jnp.swapaxes(b_re, 1, 2), jnp.swapaxes(b_im, 1, 2), c_re, c_im,
      jnp.tile(d_skip, (1, S5_L)).reshape(g, 1, gw))


def _s5_weights_kernel(p_ref, t_ref, qt_ref, perm_ref, bp_out, bt_out, bq_out, bd_ref):
    perm = perm_ref[...]
    gw = p_ref.shape[1]

    def block_diag(blocks):
        bd_ref[...] = jnp.zeros_like(bd_ref)
        for h, blk in enumerate(blocks):
            bd_ref[h * gw:(h + 1) * gw, h * gw:(h + 1) * gw] = blk.astype(BF16)
        return bd_ref[...]

    n = p_ref.shape[0]
    bp_out[...] = _dot(perm, block_diag([p_ref[h] for h in range(n)])).astype(BF16)
    rows_ok = _dot(perm, block_diag([t_ref[h] for h in range(n)])).astype(BF16)
    bt_out[...] = _dot_nt(rows_ok, perm).astype(BF16)
    bq_out[...] = _dot_nt(block_diag([qt_ref[h].T for h in range(n)]), perm).astype(BF16)


def _s5_weights(p_all, t_all, qt_all):
    g, gw, sw = p_all.shape
    dst = jnp.arange(S5_W)
    j, h, a = dst // LANES, (dst % LANES) // S5_CH, dst % S5_CH
    src = h * gw + j * S5_CH + a
    perm = (src[:, None] == jnp.arange(S5_W)[None, :]).astype(BF16)
    grp = lambda n: pl.BlockSpec((S5_QG, gw, n), lambda q: (q, 0, 0))
    big = lambda r, c: pl.BlockSpec((None, r, c), lambda q: (q, 0, 0))
    return pl.pallas_call(
        _s5_weights_kernel,
        grid=(S5_NQ,),
        in_specs=[grp(sw), grp(gw), grp(sw), pl.BlockSpec((S5_W, S5_W), lambda q: (0, 0))],
        out_specs=[big(S5_W, S5_SW), big(S5_W, S5_W), big(S5_SW, S5_W)],
        out_shape=[jax.ShapeDtypeStruct((S5_NQ, S5_W, S5_SW), BF16),
                   jax.ShapeDtypeStruct((S5_NQ, S5_W, S5_W), BF16),
                   jax.ShapeDtypeStruct((S5_NQ, S5_SW, S5_W), BF16)],
        scratch_shapes=[pltpu.VMEM((S5_W, S5_W), BF16)],
        compiler_params=pltpu.CompilerParams(dimension_semantics=("arbitrary",),
                                             vmem_limit_bytes=VMEM_LIMIT),
    )(p_all, t_all, qt_all, perm)


def _s5_kernel(u_ref, bp_ref, bt_ref, bq_ref, lr_ref, li_ref, y_ref, z_ref, s_ref):
    @pl.when(pl.program_id(0) == 0)
    def _():
        s_ref[...] = jnp.zeros_like(s_ref)

    nb, ct, _ = u_ref.shape
    rows = nb * ct
    kb = S5_SW // LANES

    def cols(q):
        return [slice(tl * D_S5 + q * LANES, tl * D_S5 + (q + 1) * LANES) for tl in range(S5_L)]

    def x_of(q):
        return jnp.concatenate([u_ref[:, :, cs].reshape(rows, LANES) for cs in cols(q)], axis=1)

    def srow(q, k):
        return pl.ds(q * kb + k, rows, stride=S5_SROWS)

    for q in range(S5_NQ):
        z = _dot(x_of(q), bp_ref[q])
        for k in range(kb):
            z_ref[srow(q, k), :] = z[:, k * LANES:(k + 1) * LANES]

    lr = lr_ref[...]
    li = li_ref[...]

    def step(c, s):
        at = [pl.ds(pl.multiple_of((b * ct + c) * S5_SROWS, S5_SROWS), S5_SROWS) for b in range(nb)]
        z = jnp.concatenate([z_ref[at[b], :] for b in range(nb)], axis=0)
        for b in range(nb):
            z_ref[at[b], :] = s[b * S5_SROWS:(b + 1) * S5_SROWS]
        return lr * s + li * pltpu.roll(s, S5_STATE, axis=1) + z

    s_ref[...] = lax.fori_loop(0, ct, step, s_ref[...])

    for q in range(S5_NQ):
        start = jnp.concatenate([z_ref[srow(q, k), :] for k in range(kb)], axis=1).astype(BF16)
        y = (_dot(x_of(q), bt_ref[q]) + _dot(start, bq_ref[q])).astype(y_ref.dtype)
        for tl, cs in enumerate(cols(q)):
            y_ref[:, :, cs] = y[:, tl * LANES:(tl + 1) * LANES].reshape(nb, ct, LANES)


def _s5_core(u2, a_re, a_im, log_dt, b_re, b_im, c_re, c_im, d_skip, ct):
    bsz, nck, _ = u2.shape
    p, ll, nq = S5_STATE, S5_L, S5_NQ
    qt_all, p_all, t_all, lam = _s5_prep(a_re, a_im, log_dt, b_re, b_im, c_re, c_im, d_skip)
    big_p, big_t, big_q = _s5_weights(p_all, t_all, qt_all)
    lam_re = lam[:, 0, :p]
    lam_im = lam[:, 0, p:]
    lr = jnp.tile(jnp.concatenate([lam_re, lam_re], axis=1), (bsz, 1))
    li = jnp.tile(jnp.concatenate([-lam_im, lam_im], axis=1), (bsz, 1))

    once = dict(pipeline_mode=pl.Buffered(1))
    const = lambda shape: pl.BlockSpec(shape, lambda i: (0,) * len(shape), **once)
    tok = pl.BlockSpec((bsz, ct, ll * D_S5), lambda i: (0, i, 0))
    y2 = pl.pallas_call(
        _s5_kernel,
        grid=(nck // ct,),
        in_specs=[tok, const((nq, S5_W, S5_SW)), const((nq, S5_W, S5_W)), const((nq, S5_SW, S5_W)),
                  const((bsz * S5_SROWS, LANES)), const((bsz * S5_SROWS, LANES))],
        out_specs=tok,
        out_shape=jax.ShapeDtypeStruct((bsz, nck, ll * D_S5), BF16),
        scratch_shapes=[pltpu.VMEM((bsz * ct * S5_SROWS, LANES), F32),
                        pltpu.VMEM((bsz * S5_SROWS, LANES), F32)],
        compiler_params=pltpu.CompilerParams(dimension_semantics=("arbitrary",),
                                             vmem_limit_bytes=VMEM_LIMIT),
    )(u2, big_p, big_t, big_q, lr, li)
    return y2


def _tail_kernel(x_ref, yr_ref, ys_ref, gm_ref, shf_ref, scf_ref, gf_ref, wglu_ref, bglu_ref, gain_ref,
                 wo_ref, wg_ref, wu_ref, wd_ref, fg_ref, o_ref, ys_scr, *, ff_tile):
    tm = x_ref.shape[0]
    for tl in range(S5_L):
        for k in range(S5_NQ):
            lo = tl * D_S5 + k * LANES
            ys_scr[k, pl.ds(tl, tm // S5_L, stride=S5_L), :] = ys_ref[:, lo:lo + LANES].astype(F32)
    ys = jnp.concatenate([ys_scr[k] for k in range(S5_NQ)], axis=1)
    zz = 0.5 * ys * (1.0 + jnp.tanh(math.sqrt(2.0 / math.pi) * (ys + 0.044715 * (ys * ys * ys))))
    gl = zz * _sigmoid(_dot(zz.astype(BF16), wglu_ref[...]) + bglu_ref[...])
    gl = gl * lax.rsqrt(jnp.mean(gl * gl, axis=-1, keepdims=True) + NORM_EPS) * gain_ref[...]
    mix = _dot(yr_ref[...].astype(BF16), wo_ref[:D_RWKV, :]) + _dot(gl.astype(BF16), wo_ref[D_RWKV:, :])
    x1 = x_ref[...] + gm_ref[...] * mix
    h = x1 * lax.rsqrt(jnp.mean(x1 * x1, axis=-1, keepdims=True) + NORM_EPS)
    h = (h * (1.0 + scf_ref[...]) + shf_ref[...]).astype(BF16)
    acc = jnp.zeros_like(x1)
    for j in range(D_FF // ff_tile):
        cs = slice(j * ff_tile, (j + 1) * ff_tile)
        gate = _dot(h, wg_ref[:, cs])
        up = _dot(h, wu_ref[:, cs])
        act = gate * _sigmoid(gate) * up
        acc = acc + _dot(act.astype(BF16), wd_ref[cs, :])
    x2 = x1 + gf_ref[...] * acc
    o_ref[...] = x2 * lax.rsqrt(jnp.mean(x2 * x2, axis=-1, keepdims=True) + NORM_EPS) * fg_ref[...]


def _tail(x, y_rwkv, y_s5, g_m, sh_f, sc_f, g_f, w_glu, b_glu, gain, w_out, w_gate, w_up, w_down,
          final_gain, tm, ff_tile):
    bsz, seq, d = x.shape
    once = dict(pipeline_mode=pl.Buffered(1))
    const = lambda shape: pl.BlockSpec(shape, lambda b, t: (0,) * len(shape), **once)
    tok = lambda n: pl.BlockSpec((None, tm, n), lambda b, t: (b, t, 0))
    per_b = pl.BlockSpec((None, 1, d), lambda b, t: (b, 0, 0))
    b3 = lambda a: a.reshape(bsz, 1, d)
    return pl.pallas_call(
        functools.partial(_tail_kernel, ff_tile=ff_tile),
        grid=(bsz, seq // tm),
        in_specs=[tok(d), tok(D_RWKV), pl.BlockSpec((None, tm // S5_L, S5_L * D_S5), lambda b, t: (b, t, 0)),
                  per_b, per_b, per_b, per_b,
                  const((D_S5, D_S5)), const((1, D_S5)), const((1, D_S5)),
                  const((D_RWKV + D_S5, d)), const((d, D_FF)), const((d, D_FF)), const((D_FF, d)),
                  const((1, d))],
        out_specs=tok(d),
        out_shape=jax.ShapeDtypeStruct((bsz, seq, d), F32),
        scratch_shapes=[pltpu.VMEM((S5_NQ, tm, LANES), F32)],
        compiler_params=pltpu.CompilerParams(dimension_semantics=("parallel", "parallel"),
                                             vmem_limit_bytes=VMEM_LIMIT),
    )(x, y_rwkv, y_s5, b3(g_m), b3(sh_f), b3(sc_f), b3(g_f), w_glu.astype(BF16), b_glu.reshape(1, -1),
      gain.reshape(1, -1), w_out.astype(BF16), w_gate.astype(BF16), w_up.astype(BF16),
      w_down.astype(BF16), final_gain.reshape(1, d))


def kernel(x, c, w_ada, b_ada, w_in, mu_shift, rw_w0, rw_w2, rw_a0, rw_a2, rw_g2, rw_k_k, rw_k_a, rw_r_k,
           rw_lnx_w, rw_lnx_b, s5_a_re, s5_a_im, s5_log_dt, s5_b_re, s5_b_im, s5_c_re, s5_c_im, s5_d,
           s5_w_glu, s5_b_glu, s5_gain, w_out, ffn_w_gate, ffn_w_up, ffn_w_down, final_gain):
    assert w_ada.shape[0] == 1, "single-layer trunk"
    seq = x.shape[1]
    ada = _ada(c, w_ada[0], b_ada[0])
    sh_m, sc_m, g_m, sh_f, sc_f, g_f = jnp.split(ada, 6, axis=-1)
    r, k, v, ld, a, b, g, u = _inproj(x, sh_m, sc_m, w_in[0], mu_shift[0], rw_w0[0], rw_w2[0], rw_a0[0],
                                      rw_a2[0], rw_g2[0], rw_k_k[0], rw_k_a[0], tm=min(256, seq))
    y_rwkv = _rwkv(r, k, v, ld, a, b, g, rw_r_k[0], rw_lnx_w[0], rw_lnx_b[0], tb=min(512, seq))
    y_s5 = _s5_core(u, s5_a_re[0], s5_a_im[0], s5_log_dt[0], s5_b_re[0], s5_b_im[0], s5_c_re[0],
                    s5_c_im[0], s5_d[0], ct=min(128, seq // S5_L))
    return _tail(x, y_rwkv, y_s5, g_m, sh_f, sc_f, g_f, s5_w_glu[0], s5_b_glu[0], s5_gain[0], w_out[0],
                 ffn_w_gate[0], ffn_w_up[0], ffn_w_down[0], final_gain, tm=min(512, seq), ff_tile=256)
```

```python
import functools
import math

import jax
import jax.numpy as jnp
from jax import lax
from jax.experimental import pallas as pl
from jax.experimental.pallas import tpu as pltpu

F32 = jnp.float32
BF16 = jnp.bfloat16

D_MODEL = 1024
D_RWKV = 512
D_S5 = 512
HEAD = 64
LORA_W = 64
LORA_A = 64
LORA_G = 128
S5_CH = 16
S5_GROUPS = 32
S5_STATE = 64
D_FF = 2816
D_SHIFT = 3 * D_RWKV + LORA_W + LORA_A + LORA_G
D_IN = D_SHIFT + D_S5
NORM_EPS = 1e-6
LNX_EPS = 64e-5

CHUNK = 64
PAIR = 2 * HEAD
INPROJ_SUB = 128
UNIT_CHUNKS = 4
S5_L = 8
LANES = 128
S5_QG = LANES // S5_CH
S5_NQ = D_S5 // LANES
S5_W = S5_L * LANES
S5_SW = S5_QG * 2 * S5_STATE
S5_SROWS = S5_NQ * S5_SW // LANES
VMEM_LIMIT = 56 * 1024 * 1024


def _split_bf16(x, n):
    parts = []
    rem = x
    for i in range(n):
        p = rem.astype(BF16)
        parts.append(p)
        if i + 1 < n:
            rem = rem - p.astype(F32)
    return parts


def _dot(a, b):
    return jnp.dot(a, b, preferred_element_type=F32)


def _dot_nt(a, b):
    return lax.dot_general(a, b, (((1,), (1,)), ((), ())), preferred_element_type=F32)


def _dot_split_lhs(x, rhs_bf16, n):
    acc = None
    for p in _split_bf16(x, n):
        d = _dot(p, rhs_bf16)
        acc = d if acc is None else acc + d
    return acc


def _sigmoid(x):
    return 1.0 / (1.0 + jnp.exp(-x))


def _ada_kernel(c_ref, w_ref, b_ref, o_ref):
    c = c_ref[...]
    act = c * _sigmoid(c)
    o_ref[...] = jnp.dot(act, w_ref[...], preferred_element_type=F32,
                         precision=lax.Precision.HIGHEST) + b_ref[...]


def _ada(c, w_ada, b_ada):
    bsz, d = c.shape
    rows = 8
    c_pad = jnp.zeros((rows, d), F32).at[:bsz].set(c)
    n_out = w_ada.shape[1]
    out = pl.pallas_call(
        _ada_kernel,
        grid=(n_out // d,),
        in_specs=[
            pl.BlockSpec((rows, d), lambda j: (0, 0)),
            pl.BlockSpec((d, d), lambda j: (0, j)),
            pl.BlockSpec((1, d), lambda j: (0, j)),
        ],
        out_specs=pl.BlockSpec((rows, d), lambda j: (0, j)),
        out_shape=jax.ShapeDtypeStruct((rows, n_out), F32),
        compiler_params=pltpu.CompilerParams(dimension_semantics=("arbitrary",)),
    )(c_pad, w_ada, b_ada.reshape(1, n_out))
    return out[:bsz]


def _inproj_kernel(x_ref, sh_ref, sc_ref, win_ref, mu_ref, w0_ref, w2_ref, a0_ref, a2_ref, g2_ref,
                   kk_ref, ka_ref, bd_ref,
                   r_out, k_out, v_out, ld_out, a_out, b_out, g_out, u_out, carry_ref, u_scr):
    @pl.when(pl.program_id(1) == 0)
    def _():
        carry_ref[...] = jnp.zeros_like(carry_ref)

    sub = INPROJ_SUB
    prev_last = carry_ref[...]
    for s in range(x_ref.shape[0] // sub):
        rows = slice(s * sub, (s + 1) * sub)
        crow = slice(s * (sub // S5_L), (s + 1) * (sub // S5_L))
        prev_last = _inproj_rows(x_ref[rows, :], prev_last, rows, crow, sh_ref, sc_ref, win_ref, mu_ref,
                                 w0_ref, w2_ref, a0_ref, a2_ref, g2_ref, kk_ref, ka_ref, bd_ref,
                                 r_out, k_out, v_out, ld_out, a_out, b_out, g_out, u_out, u_scr)
    carry_ref[...] = prev_last


def _inproj_rows(x, prev_last, rows, crow, sh_ref, sc_ref, win_ref, mu_ref, w0_ref, w2_ref, a0_ref, a2_ref,
                 g2_ref, kk_ref, ka_ref, bd_ref, r_out, k_out, v_out, ld_out, a_out, b_out, g_out, u_out, u_scr):
    tm = x.shape[0]
    ms = jnp.mean(x * x, axis=-1, keepdims=True)
    h = x * lax.rsqrt(ms + NORM_EPS)
    h = h * (1.0 + sc_ref[...]) + sh_ref[...]
    proj = _dot(h.astype(BF16), win_ref[...])
    z = proj[:, :D_SHIFT]
    for k in range(S5_NQ):
        u_scr[k, rows, :] = proj[:, D_SHIFT + k * LANES:D_SHIFT + (k + 1) * LANES]
    for tl in range(S5_L):
        for k in range(S5_NQ):
            lo = tl * D_S5 + k * LANES
            picked = u_scr[k, pl.ds(rows.start + tl, tm // S5_L, stride=S5_L), :]
            u_out[crow, lo:lo + LANES] = picked.astype(u_out.dtype)

    z_roll = pltpu.roll(z, 1, axis=0)
    row = lax.broadcasted_iota(jnp.int32, z.shape, 0)
    z_prev = jnp.where(row == 0, prev_last, z_roll)
    zz = z + mu_ref[...] * (z_prev - z)

    r = zz[:, 0:D_RWKV]
    k = zz[:, D_RWKV:2 * D_RWKV]
    v = zz[:, 2 * D_RWKV:3 * D_RWKV]
    o = 3 * D_RWKV
    w_lo = zz[:, o:o + LORA_W]
    a_lo = zz[:, o + LORA_W:o + LORA_W + LORA_A]
    g_lo = zz[:, o + LORA_W + LORA_A:o + LORA_W + LORA_A + LORA_G]

    wl = w0_ref[...] + _dot(jnp.tanh(w_lo).astype(BF16), w2_ref[...])
    sp = jnp.maximum(-wl, 0.0) + jnp.log(1.0 + jnp.exp(-jnp.abs(wl)))
    ld_out[rows, :] = -jnp.exp(-sp - 0.5)
    a = _sigmoid(a0_ref[...] + _dot(a_lo.astype(BF16), a2_ref[...]))
    g_out[rows, :] = _dot(_sigmoid(g_lo).astype(BF16), g2_ref[...]).astype(g_out.dtype)

    kk = k * kk_ref[...]
    n2 = _dot((kk * kk).astype(BF16), bd_ref[...])
    kk = kk / jnp.maximum(jnp.sqrt(n2), 1e-12)
    r_out[rows, :] = r.astype(r_out.dtype)
    k_out[rows, :] = (k * (1.0 + (a - 1.0) * ka_ref[...])).astype(k_out.dtype)
    v_out[rows, :] = v.astype(v_out.dtype)
    a_out[rows, :] = (-kk).astype(a_out.dtype)
    b_out[rows, :] = (kk * a).astype(b_out.dtype)
    return z[tm - 1:tm, :]


def _head_block_diag(n, dtype, value=1.0):
    i = jnp.arange(n) // HEAD
    return jnp.where(i[:, None] == i[None, :], value, 0.0).astype(dtype)


def _inproj(x, sh_m, sc_m, w_in, mu, w0, w2, a0, a2, g2, k_k, k_a, tm):
    bsz, seq, d = x.shape
    row = lambda a: a.reshape(1, -1)
    const = lambda shape: pl.BlockSpec(shape, lambda b, t: (0,) * len(shape), pipeline_mode=pl.Buffered(1))
    tok = lambda n: pl.BlockSpec((None, tm, n), lambda b, t: (b, t, 0))
    per_b = pl.BlockSpec((None, 1, d), lambda b, t: (b, 0, 0))
    out_sds = lambda dt: jax.ShapeDtypeStruct((bsz, seq, D_RWKV), dt)
    out_dtypes = [BF16, BF16, BF16, F32, BF16, BF16, BF16]
    return pl.pallas_call(
        _inproj_kernel,
        grid=(bsz, seq // tm),
        in_specs=[tok(d), per_b, per_b, const((d, D_IN)), const((1, D_SHIFT)),
                  const((1, D_RWKV)), const((LORA_W, D_RWKV)), const((1, D_RWKV)),
                  const((LORA_A, D_RWKV)), const((LORA_G, D_RWKV)), const((1, D_RWKV)),
                  const((1, D_RWKV)), const((D_RWKV, D_RWKV))],
        out_specs=[tok(D_RWKV)] * 7 + [pl.BlockSpec((None, tm // S5_L, S5_L * D_S5), lambda b, t: (b, t, 0))],
        out_shape=[out_sds(dt) for dt in out_dtypes]
        + [jax.ShapeDtypeStruct((bsz, seq // S5_L, S5_L * D_S5), BF16)],
        scratch_shapes=[pltpu.VMEM((1, D_SHIFT), F32), pltpu.VMEM((S5_NQ, tm, LANES), F32)],
        compiler_params=pltpu.CompilerParams(dimension_semantics=("parallel", "arbitrary"),
                                             vmem_limit_bytes=VMEM_LIMIT),
    )(x, sh_m.reshape(bsz, 1, d), sc_m.reshape(bsz, 1, d), w_in.astype(BF16), row(mu),
      row(w0), w2.astype(BF16), row(a0), a2.astype(BF16), g2.astype(BF16), row(k_k), row(k_a),
      _head_block_diag(D_RWKV, BF16))


def _rwkv_chunk_maps(slabs, cst):
    row, strict, incl, bd, eye, m0, m1, m0w, m1w, eye_c, merge_masks = cst
    c = CHUNK
    r, k, v, ld, a, b = ([s[i] for s in slabs] for i in range(6))
    cum = ld
    sft = 1
    while sft < c:
        if sft < 8:
            cum = [x + jnp.where(row >= sft, pltpu.roll(x, sft, axis=0), 0.0) for x in cum]
        else:
            pad = jnp.zeros((sft, PAIR), F32)
            cum = [x + jnp.concatenate([pad, x[:c - sft]], axis=0) for x in cum]
        sft *= 2
    cl = [x[c - 1:c, :] for x in cum]
    w_to = [jnp.exp(x) for x in cum]
    w_inv = [jnp.exp(-x) for x in cum]
    w_prev = [jnp.exp(x - y) for x, y in zip(cum, ld)]
    w_rem = [jnp.exp(y - x) for x, y in zip(cum, cl)]
    w_all = [jnp.exp(y) for y in cl]
    rt = [x * w for x, w in zip(r, w_to)]
    kt = [x * w for x, w in zip(k, w_inv)]
    at = [x * w for x, w in zip(a, w_prev)]
    bt = [x * w for x, w in zip(b, w_inv)]
    bh = [x * w for x, w in zip(b, w_rem)]
    kh = [x * w for x, w in zip(k, w_rem)]

    def per_head(yb, lo, hi):
        return jnp.concatenate([yb * lo, yb * hi], axis=0)

    lhs = [jnp.concatenate([x, y], axis=0).astype(BF16) for x, y in zip(at, rt)]
    rhs = [jnp.concatenate([per_head(x.astype(BF16), m0, m1), per_head(y.astype(BF16), m0, m1)], axis=0)
           for x, y in zip(bt, kt)]
    a_all = [_dot_nt(x, y) for x, y in zip(lhs, rhs)]
    n_ab = [jnp.where(strict, x[:c, :PAIR], 0.0) for x in a_all]
    a_ak = [jnp.where(strict, x[:c, PAIR:], 0.0) for x in a_all]
    a_rb = [jnp.where(incl, x[c:, :PAIR], 0.0) for x in a_all]
    a_rk = [jnp.where(incl, x[c:, PAIR:], 0.0) for x in a_all]

    vb = [x.astype(BF16) for x in v]
    vs = [per_head(x, m0, m1) for x in vb]
    x0 = [jnp.concatenate([y, _dot(z.astype(BF16), w)], axis=1) for y, z, w in zip(at, a_ak, vs)]

    nb = [x.astype(BF16) for x in n_ab]
    t_inv = [eye_c + x * merge_masks[0] for x in n_ab]
    for msk in merge_masks[1:]:
        tb = [t.astype(BF16) for t in t_inv]
        tmp = [_dot(t, per_head(n * msk, m0, m1)) for t, n in zip(tb, nb)]
        t_inv = [t + _dot(m.astype(BF16), per_head(h, m0, m1)) for t, m, h in zip(t_inv, tmp, tb)]
    x1 = [_dot(t.astype(BF16), per_head(y.astype(BF16), m0w, m1w)) for t, y in zip(t_inv, x0)]
    x1b = [x.astype(BF16) for x in x1]
    rx = [_dot(z.astype(BF16), per_head(y, m0w, m1w)) for z, y in zip(a_rb, x1b)]
    r_hat = [x + y[:, :PAIR] for x, y in zip(rt, rx)]
    y_hat = [y[:, PAIR:] + _dot(z.astype(BF16), w) for y, z, w in zip(rx, a_rk, vs)]
    g1 = [_dot(x.T.astype(BF16), y) for x, y in zip(bh, x1b)]
    g2 = [_dot(x.T.astype(BF16), y) for x, y in zip(kh, vb)]
    m_mat = [eye * w + x[:, :PAIR] * bd for w, x in zip(w_all, g1)]
    s_hat = [(x[:, PAIR:] + y) * bd for x, y in zip(g1, g2)]
    return list(zip(r_hat, y_hat, m_mat, s_hat))


def _rwkv_kernel(r_ref, k_ref, v_ref, ld_ref, a_ref, b_ref, g_ref, rk_ref, lw_ref, lb_ref,
                 bd_ref, avg_ref, y_ref, s_ref):
    t = pl.program_id(1)

    @pl.when(t == 0)
    def _():
        s_ref[...] = jnp.zeros_like(s_ref)

    c = CHUNK
    lane = lax.broadcasted_iota(jnp.int32, (1, PAIR), 1)
    m0 = (lane < HEAD).astype(F32).astype(BF16)
    m1 = (lane >= HEAD).astype(F32).astype(BF16)
    lane_w = lax.broadcasted_iota(jnp.int32, (1, 2 * PAIR), 1) % PAIR
    m0w = (lane_w < HEAD).astype(F32).astype(BF16)
    m1w = (lane_w >= HEAD).astype(F32).astype(BF16)
    row = lax.broadcasted_iota(jnp.int32, (c, PAIR), 0)
    col = lax.broadcasted_iota(jnp.int32, (c, PAIR), 1) % c
    strict = col < row
    incl = col <= row
    bd = bd_ref[...]
    ri = lax.broadcasted_iota(jnp.int32, (PAIR, PAIR), 0)
    ci = lax.broadcasted_iota(jnp.int32, (PAIR, PAIR), 1)
    eye = (ri == ci).astype(F32)
    eye_c = (col == row).astype(F32)
    merge_masks = []
    sz = 1
    while sz < c:
        msk = (row // (2 * sz) == col // (2 * sz)) & (row % (2 * sz) >= sz) & (col % (2 * sz) < sz)
        merge_masks.append(msk.astype(F32) if sz == 1 else msk.astype(F32).astype(BF16))
        sz *= 2
    cst = (row, strict, incl, bd, eye, m0, m1, m0w, m1w, eye_c, merge_masks)
    avg = avg_ref[...]
    ones_bd = bd.astype(BF16)

    n_pairs = D_RWKV // PAIR
    span = UNIT_CHUNKS * c
    units = [(ch, p) for ch in range(UNIT_CHUNKS) for p in range(n_pairs)]

    def body(i, carry):
        base = pl.multiple_of(i * span, span)
        rows = [pl.ds(base + ch * c, c) for ch, _ in units]
        lanes = [slice(p * PAIR, (p + 1) * PAIR) for _, p in units]
        slabs = [tuple(ref[rs, ls].astype(F32) for ref in (r_ref, k_ref, v_ref, ld_ref, a_ref, b_ref))
                 for rs, ls in zip(rows, lanes)]
        maps = _rwkv_chunk_maps(slabs, cst)

        state = [s_ref[p] for p in range(n_pairs)]
        ys = []
        for (ch, p), (r_hat, y_hat, m_mat, s_hat) in zip(units, maps):
            sb = state[p].astype(BF16)
            ym = _dot(jnp.concatenate([r_hat, m_mat], axis=0).astype(BF16), sb)
            ys.append(ym[:c] + y_hat)
            state[p] = ym[c:] + s_hat
        for p in range(n_pairs):
            s_ref[p] = state[p]

        y_all = jnp.concatenate(ys, axis=0)
        dlt = y_all - _dot_split_lhs(y_all, avg, 2)
        var = _dot((dlt * dlt).astype(BF16), avg)
        rk = jnp.concatenate([s[0] * s[1] * rk_ref[:, ls] for s, ls in zip(slabs, lanes)], axis=0)
        bonus = _dot(rk.astype(BF16), ones_bd)
        yn = dlt * lax.rsqrt(var + LNX_EPS)
        for n, (s, rs, ls) in enumerate(zip(slabs, rows, lanes)):
            un = slice(n * c, (n + 1) * c)
            y = yn[un] * lw_ref[:, ls] + lb_ref[:, ls] + bonus[un] * s[2]
            y_ref[rs, ls] = (y * g_ref[rs, ls].astype(F32)).astype(y_ref.dtype)
        return carry

    lax.fori_loop(0, r_ref.shape[0] // span, body, 0)


def _rwkv(r, k, v, ld, a, b, g, r_k, lnx_w, lnx_b, tb):
    bsz, seq, d = r.shape
    tok = pl.BlockSpec((None, tb, d), lambda i, t: (i, t, 0))
    const = lambda shape: pl.BlockSpec(shape, lambda i, t: (0,) * len(shape))
    return pl.pallas_call(
        _rwkv_kernel,
        grid=(bsz, seq // tb),
        in_specs=[tok] * 7 + [const((1, d))] * 3 + [const((PAIR, PAIR)), const((PAIR, PAIR))],
        out_specs=tok,
        out_shape=jax.ShapeDtypeStruct((bsz, seq, d), BF16),
        scratch_shapes=[pltpu.VMEM((d // PAIR, PAIR, PAIR), F32)],
        compiler_params=pltpu.CompilerParams(dimension_semantics=("parallel", "arbitrary"),
                                             vmem_limit_bytes=VMEM_LIMIT),
    )(r, k, v, ld, a, b, g, r_k.reshape(1, d), lnx_w.reshape(1, d), lnx_b.reshape(1, d),
      _head_block_diag(PAIR, F32), _head_block_diag(PAIR, BF16, 1.0 / HEAD))


def _s5_prep_kernel(are_ref, aim_ref, ldt_ref, btre_ref, btim_ref, cre_ref, cim_ref, dsk_ref,
                    qt_out, p_out, t_out, lam_out):
    a_re = are_ref[...]
    a_im = aim_ref[...]
    dt = jnp.exp(ldt_ref[...])
    nd = 24
    dpow = lax.broadcasted_iota(jnp.int32, (nd, S5_STATE), 0).astype(F32)
    mag = jnp.exp(dpow * (dt * a_re))
    ang = dpow * (dt * a_im)
    e_re = mag * jnp.cos(ang)
    e_im = mag * jnp.sin(ang)
    lam_re = e_re[1:2, :]
    lam_im = e_im[1:2, :]
    den = a_re * a_re + a_im * a_im
    pp = lam_re - 1.0
    qq = lam_im
    coef_re = (pp * a_re + qq * a_im) / den
    coef_im = (qq * a_re - pp * a_im) / den
    bt_re = btre_ref[...]
    bt_im = btim_ref[...]
    bb_re = coef_re * bt_re - coef_im * bt_im
    bb_im = coef_re * bt_im + coef_im * bt_re
    c_re = cre_ref[...]
    c_im = cim_ref[...]
    c_lam = []
    for d in range(S5_L + 1):
        er = e_re[d:d + 1, :]
        ei = e_im[d:d + 1, :]
        c_lam.append(jnp.concatenate([c_re * er - c_im * ei, -(c_re * ei + c_im * er)], axis=1))
    qt_out[...] = jnp.concatenate(c_lam[1:], axis=0)
    for j in range(S5_L):
        d = S5_L - 1 - j
        er = e_re[d:d + 1, :]
        ei = e_im[d:d + 1, :]
        p_out[j * S5_CH:(j + 1) * S5_CH, :] = jnp.concatenate(
            [er * bb_re - ei * bb_im, er * bb_im + ei * bb_re], axis=1)
    y0 = jnp.concatenate([bb_re, bb_im], axis=1)
    for j in range(S5_L):
        lagged = jnp.concatenate([jnp.zeros_like(c_lam[0])] * j + c_lam[:S5_L - j], axis=0)
        t_out[j * S5_CH:(j + 1) * S5_CH, :] = lax.dot_general(
            y0, lagged, (((1,), (1,)), ((), ())), preferred_element_type=F32, precision=lax.Precision.HIGHEST)
    gw = S5_L * S5_CH
    on_diag = lax.broadcasted_iota(jnp.int32, (gw, gw), 0) == lax.broadcasted_iota(jnp.int32, (gw, gw), 1)
    t_out[...] = t_out[...] + jnp.where(on_diag, dsk_ref[...], 0.0)
    lam_out[...] = jnp.concatenate([e_re[S5_L:S5_L + 1, :], e_im[S5_L:S5_L + 1, :]], axis=1)


def _s5_prep(a_re, a_im, log_dt, b_re, b_im, c_re, c_im, d_skip):
    g, p = a_re.shape
    per_g = lambda shape: pl.BlockSpec((None,) + shape, lambda i: (i,) + (0,) * len(shape))
    gw = S5_L * S5_CH
    return pl.pallas_call(
        _s5_prep_kernel,
        grid=(g,),
        in_specs=[per_g((1, p)), per_g((1, p)), per_g((1, 1)), per_g((S5_CH, p)), per_g((S5_CH, p)),
                  per_g((S5_CH, p)), per_g((S5_CH, p)), per_g((1, gw))],
        out_specs=[per_g((gw, 2 * p)), per_g((gw, 2 * p)), per_g((gw, gw)), per_g((1, 2 * p))],
        out_shape=[jax.ShapeDtypeStruct((g, gw, 2 * p), F32),
                   jax.ShapeDtypeStruct((g, gw, 2 * p), F32),
                   jax.ShapeDtypeStruct((g, gw, gw), F32),
                   jax.ShapeDtypeStruct((g, 1, 2 * p), F32)],
        compiler_params=pltpu.CompilerParams(dimension_semantics=("arbitrary",)),
    )(a_re.reshape(g, 1, p), a_im.reshape(g, 1, p), log_dt.reshape(g, 1, 1),
      jnp.swapaxes(b_re, 1, 2), jnp.swapaxes(b_im, 1, 2), c_re, c_im,
      jnp.tile(d_skip, (1, S5_L)).reshape(g, 1, gw))


def _s5_weights_kernel(p_ref, t_ref, qt_ref, perm_ref, bp_out, bt_out, bq_out, bd_ref):
    perm = perm_ref[...]
    gw = p_ref.shape[1]

    def block_diag(blocks):
        bd_ref[...] = jnp.zeros_like(bd_ref)
        for h, blk in enumerate(blocks):
            bd_ref[h * gw:(h + 1) * gw, h * gw:(h + 1) * gw] = blk.astype(BF16)
        return bd_ref[...]

    n = p_ref.shape[0]
    bp_out[...] = _dot(perm, block_diag([p_ref[h] for h in range(n)])).astype(BF16)
    rows_ok = _dot(perm, block_diag([t_ref[h] for h in range(n)])).astype(BF16)
    bt_out[...] = _dot_nt(rows_ok, perm).astype(BF16)
    bq_out[...] = _dot_nt(block_diag([qt_ref[h].T for h in range(n)]), perm).astype(BF16)


def _s5_weights(p_all, t_all, qt_all):
    g, gw, sw = p_all.shape
    dst = jnp.arange(S5_W)
    j, h, a = dst // LANES, (dst % LANES) // S5_CH, dst % S5_CH
    src = h * gw + j * S5_CH + a
    perm = (src[:, None] == jnp.arange(S5_W)[None, :]).astype(BF16)
    grp = lambda n: pl.BlockSpec((S5_QG, gw, n), lambda q: (q, 0, 0))
    big = lambda r, c: pl.BlockSpec((None, r, c), lambda q: (q, 0, 0))
    return pl.pallas_call(
        _s5_weights_kernel,
        grid=(S5_NQ,),
        in_specs=[grp(sw), grp(gw), grp(sw), pl.BlockSpec((S5_W, S5_W), lambda q: (0, 0))],
        out_specs=[big(S5_W, S5_SW), big(S5_W, S5_W), big(S5_SW, S5_W)],
        out_shape=[jax.ShapeDtypeStruct((S5_NQ, S5_W, S5_SW), BF16),
                   jax.ShapeDtypeStruct((S5_NQ, S5_W, S5_W), BF16),
                   jax.ShapeDtypeStruct((S5_NQ, S5_SW, S5_W), BF16)],
        scratch_shapes=[pltpu.VMEM((S5_W, S5_W), BF16)],
        compiler_params=pltpu.CompilerParams(dimension_semantics=("arbitrary",),
                                             vmem_limit_bytes=VMEM_LIMIT),
    )(p_all, t_all, qt_all, perm)


def _s5_kernel(u_ref, bp_ref, bt_ref, bq_ref, lr_ref, li_ref, y_ref, z_ref, s_ref):
    @pl.when(pl.program_id(0) == 0)
    def _():
        s_ref[...] = jnp.zeros_like(s_ref)

    nb, ct, _ = u_ref.shape
    rows = nb * ct
    kb = S5_SW // LANES

    def cols(q):
        return [slice(tl * D_S5 + q * LANES, tl * D_S5 + (q + 1) * LANES) for tl in range(S5_L)]

    def x_of(q):
        return jnp.concatenate([u_ref[:, :, cs].reshape(rows, LANES) for cs in cols(q)], axis=1)

    def srow(q, k):
        return pl.ds(q * kb + k, rows, stride=S5_SROWS)

    for q in range(S5_NQ):
        z = _dot(x_of(q), bp_ref[q])
        for k in range(kb):
            z_ref[srow(q, k), :] = z[:, k * LANES:(k + 1) * LANES]

    lr = lr_ref[...]
    li = li_ref[...]

    def step(c, carry):
        s, s_sw = carry
        at = [pl.ds(pl.multiple_of((b * ct + c) * S5_SROWS, S5_SROWS), S5_SROWS) for b in range(nb)]
        z = jnp.concatenate([z_ref[at[b], :] for b in range(nb)], axis=0)
        for b in range(nb):
            z_ref[at[b], :] = s[b * S5_SROWS:(b + 1) * S5_SROWS]
        z_sw = pltpu.roll(z, S5_STATE, axis=1)
        return lr * s + li * s_sw + z, lr * s_sw - li * s + z_sw

    s0 = s_ref[...]
    s_end, _ = lax.fori_loop(0, ct, step, (s0, pltpu.roll(s0, S5_STATE, axis=1)), unroll=2)
    s_ref[...] = s_end

    for q in range(S5_NQ):
        start = jnp.concatenate([z_ref[srow(q, k), :] for k in range(kb)], axis=1).astype(BF16)
        y = (_dot(x_of(q), bt_ref[q]) + _dot(start, bq_ref[q])).astype(y_ref.dtype)
        for tl, cs in enumerate(cols(q)):
            y_ref[:, :, cs] = y[:, tl * LANES:(tl + 1) * LANES].reshape(nb, ct, LANES)


def _s5_core(u2, a_re, a_im, log_dt, b_re, b_im, c_re, c_im, d_skip, ct):
    bsz, nck, _ = u2.shape
    p, ll, nq = S5_STATE, S5_L, S5_NQ
    qt_all, p_all, t_all, lam = _s5_prep(a_re, a_im, log_dt, b_re, b_im, c_re, c_im, d_skip)
    big_p, big_t, big_q = _s5_weights(p_all, t_all, qt_all)
    lam_re = lam[:, 0, :p]
    lam_im = lam[:, 0, p:]
    lr = jnp.tile(jnp.concatenate([lam_re, lam_re], axis=1), (bsz, 1))
    li = jnp.tile(jnp.concatenate([-lam_im, lam_im], axis=1), (bsz, 1))

    once = dict(pipeline_mode=pl.Buffered(1))
    const = lambda shape: pl.BlockSpec(shape, lambda i: (0,) * len(shape), **once)
    tok = pl.BlockSpec((bsz, ct, ll * D_S5), lambda i: (0, i, 0))
    y2 = pl.pallas_call(
        _s5_kernel,
        grid=(nck // ct,),
        in_specs=[tok, const((nq, S5_W, S5_SW)), const((nq, S5_W, S5_W)), const((nq, S5_SW, S5_W)),
                  const((bsz * S5_SROWS, LANES)), const((bsz * S5_SROWS, LANES))],
        out_specs=tok,
        out_shape=jax.ShapeDtypeStruct((bsz, nck, ll * D_S5), BF16),
        scratch_shapes=[pltpu.VMEM((bsz * ct * S5_SROWS, LANES), F32),
                        pltpu.VMEM((bsz * S5_SROWS, LANES), F32)],
        compiler_params=pltpu.CompilerParams(dimension_semantics=("arbitrary",),
                                             vmem_limit_bytes=VMEM_LIMIT),
    )(u2, big_p, big_t, big_q, lr, li)
    return y2


def _tail_kernel(x_ref, yr_ref, ys_ref, gm_ref, shf_ref, scf_ref, gf_ref, wglu_ref, bglu_ref, gain_ref,
                 wo_ref, wg_ref, wu_ref, wd_ref, fg_ref, o_ref, ys_scr, act_scr, *, ff_tile):
    tm = x_ref.shape[0]
    for tl in range(S5_L):
        for k in range(S5_NQ):
            lo = tl * D_S5 + k * LANES
            ys_scr[k, pl.ds(tl, tm // S5_L, stride=S5_L), :] = ys_ref[:, lo:lo + LANES].astype(F32)
    ys = jnp.concatenate([ys_scr[k] for k in range(S5_NQ)], axis=1)
    zz = 0.5 * ys * (1.0 + jnp.tanh(math.sqrt(2.0 / math.pi) * (ys + 0.044715 * (ys * ys * ys))))
    gl = zz * _sigmoid(_dot(zz.astype(BF16), wglu_ref[...]) + bglu_ref[...])
    gl = gl * lax.rsqrt(jnp.mean(gl * gl, axis=-1, keepdims=True) + NORM_EPS) * gain_ref[...]
    mix = _dot(yr_ref[...].astype(BF16), wo_ref[:D_RWKV, :]) + _dot(gl.astype(BF16), wo_ref[D_RWKV:, :])
    x1 = x_ref[...] + gm_ref[...] * mix
    h = x1 * lax.rsqrt(jnp.mean(x1 * x1, axis=-1, keepdims=True) + NORM_EPS)
    h = (h * (1.0 + scf_ref[...]) + shf_ref[...]).astype(BF16)
    for j in range(D_FF // ff_tile):
        cs = slice(j * ff_tile, (j + 1) * ff_tile)
        gate = _dot(h, wg_ref[:, cs])
        up = _dot(h, wu_ref[:, cs])
        act_scr[:, cs] = (gate * _sigmoid(gate) * up).astype(BF16)
    x2 = x1 + gf_ref[...] * _dot(act_scr[...], wd_ref[...])
    o_ref[...] = x2 * lax.rsqrt(jnp.mean(x2 * x2, axis=-1, keepdims=True) + NORM_EPS) * fg_ref[...]


def _tail(x, y_rwkv, y_s5, g_m, sh_f, sc_f, g_f, w_glu, b_glu, gain, w_out, w_gate, w_up, w_down,
          final_gain, tm, ff_tile):
    bsz, seq, d = x.shape
    once = dict(pipeline_mode=pl.Buffered(1))
    const = lambda shape: pl.BlockSpec(shape, lambda b, t: (0,) * len(shape), **once)
    tok = lambda n: pl.BlockSpec((None, tm, n), lambda b, t: (b, t, 0))
    per_b = pl.BlockSpec((None, 1, d), lambda b, t: (b, 0, 0))
    b3 = lambda a: a.reshape(bsz, 1, d)
    return pl.pallas_call(
        functools.partial(_tail_kernel, ff_tile=ff_tile),
        grid=(bsz, seq // tm),
        in_specs=[tok(d), tok(D_RWKV), pl.BlockSpec((None, tm // S5_L, S5_L * D_S5), lambda b, t: (b, t, 0)),
                  per_b, per_b, per_b, per_b,
                  const((D_S5, D_S5)), const((1, D_S5)), const((1, D_S5)),
                  const((D_RWKV + D_S5, d)), const((d, D_FF)), const((d, D_FF)), const((D_FF, d)),
                  const((1, d))],
        out_specs=tok(d),
        out_shape=jax.ShapeDtypeStruct((bsz, seq, d), F32),
        scratch_shapes=[pltpu.VMEM((S5_NQ, tm, LANES), F32), pltpu.VMEM((tm, D_FF), BF16)],
        compiler_params=pltpu.CompilerParams(dimension_semantics=("parallel", "parallel"),
                                             vmem_limit_bytes=VMEM_LIMIT),
    )(x, y_rwkv, y_s5, b3(g_m), b3(sh_f), b3(sc_f), b3(g_f), w_glu.astype(BF16), b_glu.reshape(1, -1),
      gain.reshape(1, -1), w_out.astype(BF16), w_gate.astype(BF16), w_up.astype(BF16),
      w_down.astype(BF16), final_gain.reshape(1, d))


def kernel(x, c, w_ada, b_ada, w_in, mu_shift, rw_w0, rw_w2, rw_a0, rw_a2, rw_g2, rw_k_k, rw_k_a, rw_r_k,
           rw_lnx_w, rw_lnx_b, s5_a_re, s5_a_im, s5_log_dt, s5_b_re, s5_b_im, s5_c_re, s5_c_im, s5_d,
           s5_w_glu, s5_b_glu, s5_gain, w_out, ffn_w_gate, ffn_w_up, ffn_w_down, final_gain):
    assert w_ada.shape[0] == 1, "single-layer trunk"
    seq = x.shape[1]
    ada = _ada(c, w_ada[0], b_ada[0])
    sh_m, sc_m, g_m, sh_f, sc_f, g_f = jnp.split(ada, 6, axis=-1)
    r, k, v, ld, a, b, g, u = _inproj(x, sh_m, sc_m, w_in[0], mu_shift[0], rw_w0[0], rw_w2[0], rw_a0[0],
                                      rw_a2[0], rw_g2[0], rw_k_k[0], rw_k_a[0], tm=min(512, seq))
    y_rwkv = _rwkv(r, k, v, ld, a, b, g, rw_r_k[0], rw_lnx_w[0], rw_lnx_b[0], tb=min(512, seq))
    y_s5 = _s5_core(u, s5_a_re[0], s5_a_im[0], s5_log_dt[0], s5_b_re[0], s5_b_im[0], s5_c_re[0],
                    s5_c_im[0], s5_d[0], ct=min(128, seq // S5_L))
    return _tail(x, y_rwkv, y_s5, g_m, sh_f, sc_f, g_f, s5_w_glu[0], s5_b_glu[0], s5_gain[0], w_out[0],
                 ffn_w_gate[0], ffn_w_up[0], ffn_w_down[0], final_gain, tm=min(512, seq), ff_tile=256)
```

```python
import functools
import math

import jax
import jax.numpy as jnp
from jax import lax
from jax.experimental import pallas as pl
from jax.experimental.pallas import tpu as pltpu

F32 = jnp.float32
BF16 = jnp.bfloat16

D_MODEL = 1024
D_RWKV = 512
D_S5 = 512
HEAD = 64
LORA_W = 64
LORA_A = 64
LORA_G = 128
S5_CH = 16
S5_GROUPS = 32
S5_STATE = 64
D_FF = 2816
D_SHIFT = 3 * D_RWKV + LORA_W + LORA_A + LORA_G
D_IN = D_SHIFT + D_S5
NORM_EPS = 1e-6
LNX_EPS = 64e-5

CHUNK = 64
PAIR = 2 * HEAD
INPROJ_SUB = 128
UNIT_CHUNKS = 4
S5_L = 8
LANES = 128
S5_QG = LANES // S5_CH
S5_NQ = D_S5 // LANES
S5_W = S5_L * LANES
S5_SW = S5_QG * 2 * S5_STATE
S5_SROWS = S5_NQ * S5_SW // LANES
VMEM_LIMIT = 56 * 1024 * 1024


def _split_bf16(x, n):
    parts = []
    rem = x
    for i in range(n):
        p = rem.astype(BF16)
        parts.append(p)
        if i + 1 < n:
            rem = rem - p.astype(F32)
    return parts


def _dot(a, b):
    return jnp.dot(a, b, preferred_element_type=F32)


def _dot_nt(a, b):
    return lax.dot_general(a, b, (((1,), (1,)), ((), ())), preferred_element_type=F32)


def _dot_split_lhs(x, rhs_bf16, n):
    acc = None
    for p in _split_bf16(x, n):
        d = _dot(p, rhs_bf16)
        acc = d if acc is None else acc + d
    return acc


def _sigmoid(x):
    return 1.0 / (1.0 + jnp.exp(-x))


def _ada_kernel(c_ref, w_ref, b_ref, o_ref):
    c = c_ref[...]
    act = c * _sigmoid(c)
    o_ref[...] = jnp.dot(act, w_ref[...], preferred_element_type=F32,
                         precision=lax.Precision.HIGHEST) + b_ref[...]


def _ada(c, w_ada, b_ada):
    bsz, d = c.shape
    rows = 8
    c_pad = jnp.zeros((rows, d), F32).at[:bsz].set(c)
    n_out = w_ada.shape[1]
    out = pl.pallas_call(
        _ada_kernel,
        grid=(n_out // d,),
        in_specs=[
            pl.BlockSpec((rows, d), lambda j: (0, 0)),
            pl.BlockSpec((d, d), lambda j: (0, j)),
            pl.BlockSpec((1, d), lambda j: (0, j)),
        ],
        out_specs=pl.BlockSpec((rows, d), lambda j: (0, j)),
        out_shape=jax.ShapeDtypeStruct((rows, n_out), F32),
        compiler_params=pltpu.CompilerParams(dimension_semantics=("arbitrary",)),
    )(c_pad, w_ada, b_ada.reshape(1, n_out))
    return out[:bsz]


def _inproj_kernel(x_ref, sh_ref, sc_ref, win_ref, mu_ref, w0_ref, w2_ref, a0_ref, a2_ref, g2_ref,
                   kk_ref, ka_ref, bd_ref,
                   r_out, k_out, v_out, ld_out, a_out, b_out, g_out, u_out, carry_ref, u_scr):
    @pl.when(pl.program_id(1) == 0)
    def _():
        carry_ref[...] = jnp.zeros_like(carry_ref)

    sub = INPROJ_SUB
    prev_last = carry_ref[...]
    for s in range(x_ref.shape[0] // sub):
        rows = slice(s * sub, (s + 1) * sub)
        crow = slice(s * (sub // S5_L), (s + 1) * (sub // S5_L))
        prev_last = _inproj_rows(x_ref[rows, :], prev_last, rows, crow, sh_ref, sc_ref, win_ref, mu_ref,
                                 w0_ref, w2_ref, a0_ref, a2_ref, g2_ref, kk_ref, ka_ref, bd_ref,
                                 r_out, k_out, v_out, ld_out, a_out, b_out, g_out, u_out, u_scr)
    carry_ref[...] = prev_last


def _inproj_rows(x, prev_last, rows, crow, sh_ref, sc_ref, win_ref, mu_ref, w0_ref, w2_ref, a0_ref, a2_ref,
                 g2_ref, kk_ref, ka_ref, bd_ref, r_out, k_out, v_out, ld_out, a_out, b_out, g_out, u_out, u_scr):
    tm = x.shape[0]
    ms = jnp.mean(x * x, axis=-1, keepdims=True)
    h = x * lax.rsqrt(ms + NORM_EPS)
    h = h * (1.0 + sc_ref[...]) + sh_ref[...]
    proj = _dot(h.astype(BF16), win_ref[...])
    z = proj[:, :D_SHIFT]
    for k in range(S5_NQ):
        u_scr[k, rows, :] = proj[:, D_SHIFT + k * LANES:D_SHIFT + (k + 1) * LANES]
    for tl in range(S5_L):
        for k in range(S5_NQ):
            lo = tl * D_S5 + k * LANES
            picked = u_scr[k, pl.ds(rows.start + tl, tm // S5_L, stride=S5_L), :]
            u_out[crow, lo:lo + LANES] = picked.astype(u_out.dtype)

    z_roll = pltpu.roll(z, 1, axis=0)
    row = lax.broadcasted_iota(jnp.int32, z.shape, 0)
    z_prev = jnp.where(row == 0, prev_last, z_roll)
    zz = z + mu_ref[...] * (z_prev - z)

    r = zz[:, 0:D_RWKV]
    k = zz[:, D_RWKV:2 * D_RWKV]
    v = zz[:, 2 * D_RWKV:3 * D_RWKV]
    o = 3 * D_RWKV
    w_lo = zz[:, o:o + LORA_W]
    a_lo = zz[:, o + LORA_W:o + LORA_W + LORA_A]
    g_lo = zz[:, o + LORA_W + LORA_A:o + LORA_W + LORA_A + LORA_G]

    wl = w0_ref[...] + _dot(jnp.tanh(w_lo).astype(BF16), w2_ref[...])
    sp = jnp.maximum(-wl, 0.0) + jnp.log(1.0 + jnp.exp(-jnp.abs(wl)))
    ld_out[rows, :] = -jnp.exp(-sp - 0.5)
    a = _sigmoid(a0_ref[...] + _dot(a_lo.astype(BF16), a2_ref[...]))
    g_out[rows, :] = _dot(_sigmoid(g_lo).astype(BF16), g2_ref[...]).astype(g_out.dtype)

    kk = k * kk_ref[...]
    n2 = _dot((kk * kk).astype(BF16), bd_ref[...])
    kk = kk / jnp.maximum(jnp.sqrt(n2), 1e-12)
    r_out[rows, :] = r.astype(r_out.dtype)
    k_out[rows, :] = (k * (1.0 + (a - 1.0) * ka_ref[...])).astype(k_out.dtype)
    v_out[rows, :] = v.astype(v_out.dtype)
    a_out[rows, :] = (-kk).astype(a_out.dtype)
    b_out[rows, :] = (kk * a).astype(b_out.dtype)
    return z[tm - 1:tm, :]


def _head_block_diag(n, dtype, value=1.0):
    i = jnp.arange(n) // HEAD
    return jnp.where(i[:, None] == i[None, :], value, 0.0).astype(dtype)


def _inproj(x, sh_m, sc_m, w_in, mu, w0, w2, a0, a2, g2, k_k, k_a, tm):
    bsz, seq, d = x.shape
    row = lambda a: a.reshape(1, -1)
    const = lambda shape: pl.BlockSpec(shape, lambda b, t: (0,) * len(shape), pipeline_mode=pl.Buffered(1))
    tok = lambda n: pl.BlockSpec((None, tm, n), lambda b, t: (b, t, 0))
    per_b = pl.BlockSpec((None, 1, d), lambda b, t: (b, 0, 0))
    out_sds = lambda dt: jax.ShapeDtypeStruct((bsz, seq, D_RWKV), dt)
    out_dtypes = [BF16, BF16, BF16, F32, BF16, BF16, BF16]
    return pl.pallas_call(
        _inproj_kernel,
        grid=(bsz, seq // tm),
        in_specs=[tok(d), per_b, per_b, const((d, D_IN)), const((1, D_SHIFT)),
                  const((1, D_RWKV)), const((LORA_W, D_RWKV)), const((1, D_RWKV)),
                  const((LORA_A, D_RWKV)), const((LORA_G, D_RWKV)), const((1, D_RWKV)),
                  const((1, D_RWKV)), const((D_RWKV, D_RWKV))],
        out_specs=[tok(D_RWKV)] * 7 + [pl.BlockSpec((None, tm // S5_L, S5_L * D_S5), lambda b, t: (b, t, 0))],
        out_shape=[out_sds(dt) for dt in out_dtypes]
        + [jax.ShapeDtypeStruct((bsz, seq // S5_L, S5_L * D_S5), BF16)],
        scratch_shapes=[pltpu.VMEM((1, D_SHIFT), F32), pltpu.VMEM((S5_NQ, tm, LANES), F32)],
        compiler_params=pltpu.CompilerParams(dimension_semantics=("parallel", "arbitrary"),
                                             vmem_limit_bytes=VMEM_LIMIT),
    )(x, sh_m.reshape(bsz, 1, d), sc_m.reshape(bsz, 1, d), w_in.astype(BF16), row(mu),
      row(w0), w2.astype(BF16), row(a0), a2.astype(BF16), g2.astype(BF16), row(k_k), row(k_a),
      _head_block_diag(D_RWKV, BF16))


def _per_head(yb, lo, hi):
    return jnp.concatenate([yb * lo, yb * hi], axis=0)


def _rwkv_prep(slabs, rk_rows, cst):
    row, strict, incl, bd, eye, m0, m1, m0w, m1w, eye_c, merge_masks = cst
    c = CHUNK
    r, k, v, ld, a, b = ([s[i] for s in slabs] for i in range(6))
    cum = ld
    sft = 1
    while sft < c:
        if sft < 8:
            cum = [x + jnp.where(row >= sft, pltpu.roll(x, sft, axis=0), 0.0) for x in cum]
        else:
            pad = jnp.zeros((sft, PAIR), F32)
            cum = [x + jnp.concatenate([pad, x[:c - sft]], axis=0) for x in cum]
        sft *= 2
    cl = [x[c - 1:c, :] for x in cum]
    w_to = [jnp.exp(x) for x in cum]
    w_inv = [jnp.exp(-x) for x in cum]
    w_prev = [jnp.exp(x - y) for x, y in zip(cum, ld)]
    w_rem = [jnp.exp(y - x) for x, y in zip(cum, cl)]
    w_all = [jnp.exp(y) for y in cl]
    rt = [x * w for x, w in zip(r, w_to)]
    kt = [x * w for x, w in zip(k, w_inv)]
    at = [x * w for x, w in zip(a, w_prev)]
    bt = [x * w for x, w in zip(b, w_inv)]
    bh = [x * w for x, w in zip(b, w_rem)]
    kh = [x * w for x, w in zip(k, w_rem)]
    lhs = [jnp.concatenate([x, y], axis=0).astype(BF16) for x, y in zip(at, rt)]
    rhs = [jnp.concatenate([_per_head(x.astype(BF16), m0, m1), _per_head(y.astype(BF16), m0, m1)], axis=0)
           for x, y in zip(bt, kt)]
    vb = [x.astype(BF16) for x in v]
    bk_t = [jnp.concatenate([x, y], axis=0).T.astype(BF16) for x, y in zip(bh, kh)]
    rkb = [(x * y * z).astype(BF16) for x, y, z in zip(r, k, rk_rows)]
    return list(zip(lhs, rhs, vb, at, rt, bk_t, w_all, rkb))


def _rwkv_chunk_maps(ops, cst):
    row, strict, incl, bd, eye, m0, m1, m0w, m1w, eye_c, merge_masks = cst
    c = CHUNK
    lhs, rhs, vb, at, rt, bk_t, w_all, _ = ([o[i] for o in ops] for i in range(8))
    per_head = _per_head
    a_all = [_dot_nt(x, y) for x, y in zip(lhs, rhs)]
    n_ab = [jnp.where(strict, x[:c, :PAIR], 0.0) for x in a_all]
    a_ak = [jnp.where(strict, x[:c, PAIR:], 0.0) for x in a_all]
    a_rb = [jnp.where(incl, x[c:, :PAIR], 0.0) for x in a_all]
    a_rk = [jnp.where(incl, x[c:, PAIR:], 0.0) for x in a_all]

    vs = [per_head(x, m0, m1) for x in vb]
    x0 = [jnp.concatenate([y, _dot(z.astype(BF16), w)], axis=1) for y, z, w in zip(at, a_ak, vs)]

    nb = [x.astype(BF16) for x in n_ab]
    t_inv = [eye_c + x * merge_masks[0] for x in n_ab]
    anchors = [None] * len(ops)
    for lvl, msk in enumerate(merge_masks[1:]):
        tb = [t.astype(BF16) for t in t_inv]
        tmp = [_dot(t, per_head(n * msk, m0, m1)) for t, n in zip(tb, nb)]
        t_inv = [t + _dot(m.astype(BF16), per_head(h, m0, m1)) for t, m, h in zip(t_inv, tmp, tb)]
        for n in range(len(ops)):
            if lvl == (n * (len(merge_masks) - 1)) // len(ops):
                anchors[n] = tmp[n]
    x1 = [_dot(t.astype(BF16), per_head(y.astype(BF16), m0w, m1w)) for t, y in zip(t_inv, x0)]
    x1b = [x.astype(BF16) for x in x1]
    rx = [_dot(z.astype(BF16), per_head(y, m0w, m1w)) for z, y in zip(a_rb, x1b)]
    r_hat = [x + y[:, :PAIR] for x, y in zip(rt, rx)]
    y_hat = [y[:, PAIR:] + _dot(z.astype(BF16), w) for y, z, w in zip(rx, a_rk, vs)]
    zero = jnp.zeros((c, PAIR), BF16)
    gm = [_dot(x, jnp.concatenate([y, jnp.concatenate([zero, z], axis=1)], axis=0))
          for x, y, z in zip(bk_t, x1b, vb)]
    m_mat = [eye * w + x[:, :PAIR] * bd for w, x in zip(w_all, gm)]
    s_hat = [x[:, PAIR:] * bd for x in gm]
    return list(zip(r_hat, y_hat, m_mat, s_hat)), anchors


def _rwkv_kernel(r_ref, k_ref, v_ref, ld_ref, a_ref, b_ref, g_ref, rk_ref, lw_ref, lb_ref,
                 bd_ref, avg2_ref, stat2_ref, y_ref, s_ref, *prep_refs):
    t = pl.program_id(1)

    @pl.when(t == 0)
    def _():
        s_ref[...] = jnp.zeros_like(s_ref)

    c = CHUNK
    lane = lax.broadcasted_iota(jnp.int32, (1, PAIR), 1)
    m0 = (lane < HEAD).astype(F32).astype(BF16)
    m1 = (lane >= HEAD).astype(F32).astype(BF16)
    lane_w = lax.broadcasted_iota(jnp.int32, (1, 2 * PAIR), 1) % PAIR
    m0w = (lane_w < HEAD).astype(F32).astype(BF16)
    m1w = (lane_w >= HEAD).astype(F32).astype(BF16)
    row = lax.broadcasted_iota(jnp.int32, (c, PAIR), 0)
    col = lax.broadcasted_iota(jnp.int32, (c, PAIR), 1) % c
    strict = col < row
    incl = col <= row
    bd = bd_ref[...]
    ri = lax.broadcasted_iota(jnp.int32, (PAIR, PAIR), 0)
    ci = lax.broadcasted_iota(jnp.int32, (PAIR, PAIR), 1)
    eye = (ri == ci).astype(F32)
    eye_c = (col == row).astype(F32)
    merge_masks = []
    sz = 1
    while sz < c:
        msk = (row // (2 * sz) == col // (2 * sz)) & (row % (2 * sz) >= sz) & (col % (2 * sz) < sz)
        merge_masks.append(msk.astype(F32) if sz == 1 else msk.astype(F32).astype(BF16))
        sz *= 2
    cst = (row, strict, incl, bd, eye, m0, m1, m0w, m1w, eye_c, merge_masks)

    n_pairs = D_RWKV // PAIR
    span = UNIT_CHUNKS * c
    units = [(ch, p) for ch in range(UNIT_CHUNKS) for p in range(n_pairs)]
    lanes = [slice(p * PAIR, (p + 1) * PAIR) for _, p in units]
    n_blocks = r_ref.shape[0] // span
    slot_a, slot_b = prep_refs[:len(prep_refs) // 2], prep_refs[len(prep_refs) // 2:]

    def unit_rows(blk):
        base = pl.multiple_of(blk * span, span)
        return [pl.ds(base + ch * c, c) for ch, _ in units]

    def exact_zero(x):
        bits = pltpu.bitcast(x, jnp.uint32)
        half = jnp.uint32(16)
        return pltpu.bitcast(lax.shift_right_logical(lax.shift_right_logical(bits, half), half), F32)

    def prep_block(blk, slot, after=None):
        slabs = [tuple(ref[rs, ls].astype(F32) for ref in (r_ref, k_ref, v_ref, ld_ref, a_ref, b_ref))
                 for rs, ls in zip(unit_rows(blk), lanes)]
        if after is not None:
            slabs = [s[:3] + (s[3] + exact_zero(x),) + s[4:] for s, x in zip(slabs, after)]
        ops = _rwkv_prep(slabs, [rk_ref[:, ls] for ls in lanes], cst)
        for n, op in enumerate(ops):
            for ref, val in zip(slot, op):
                ref[n] = val

    def main_block(blk, slot):
        ops = [tuple(ref[n] for ref in slot) for n in range(len(units))]
        maps, anchors = _rwkv_chunk_maps(ops, cst)

        state = [s_ref[p] for p in range(n_pairs)]
        ys = []
        for (ch, p), (r_hat, y_hat, m_mat, s_hat) in zip(units, maps):
            sb = state[p].astype(BF16)
            ym = _dot(jnp.concatenate([r_hat, m_mat], axis=0).astype(BF16), sb)
            ys.append(ym[:c] + y_hat)
            state[p] = ym[c:] + s_hat
        for p in range(n_pairs):
            s_ref[p] = state[p]

        y_all = jnp.concatenate(ys, axis=0)
        dlt = y_all - _dot(jnp.concatenate(_split_bf16(y_all, 2), axis=1), avg2_ref[...])
        rkb = jnp.concatenate([op[7] for op in ops], axis=0)
        stats = _dot(jnp.concatenate([(dlt * dlt).astype(BF16), rkb], axis=1), stat2_ref[...])
        yn = dlt * lax.rsqrt(stats[:, :PAIR] + LNX_EPS)
        bonus = stats[:, PAIR:]
        for n, (op, rs, ls) in enumerate(zip(ops, unit_rows(blk), lanes)):
            un = slice(n * c, (n + 1) * c)
            y = yn[un] * lw_ref[:, ls] + lb_ref[:, ls] + bonus[un] * op[2].astype(F32)
            y_ref[rs, ls] = (y * g_ref[rs, ls].astype(F32)).astype(y_ref.dtype)
        return anchors

    prep_block(0, slot_a)

    def body(j, carry):
        done = main_block(2 * j, slot_a)
        prep_block(2 * j + 1, slot_b, after=done)
        done = main_block(2 * j + 1, slot_b)
        prep_block(jnp.minimum(2 * j + 2, n_blocks - 1), slot_a, after=done)
        return carry

    lax.fori_loop(0, n_blocks // 2, body, 0)


def _rwkv(r, k, v, ld, a, b, g, r_k, lnx_w, lnx_b, tb):
    bsz, seq, d = r.shape
    tok = pl.BlockSpec((None, tb, d), lambda i, t: (i, t, 0))
    const = lambda shape: pl.BlockSpec(shape, lambda i, t: (0,) * len(shape))
    assert (tb // (UNIT_CHUNKS * CHUNK)) % 2 == 0, "the block pipeline is written for an even block count"
    c, nu = CHUNK, UNIT_CHUNKS * (d // PAIR)
    slot = [pltpu.VMEM((nu, 2 * c, PAIR), BF16), pltpu.VMEM((nu, 4 * c, PAIR), BF16),
            pltpu.VMEM((nu, c, PAIR), BF16), pltpu.VMEM((nu, c, PAIR), F32), pltpu.VMEM((nu, c, PAIR), F32),
            pltpu.VMEM((nu, PAIR, 2 * c), BF16), pltpu.VMEM((nu, 1, PAIR), F32), pltpu.VMEM((nu, c, PAIR), BF16)]
    avg = _head_block_diag(PAIR, BF16, 1.0 / HEAD)
    ones = _head_block_diag(PAIR, BF16)
    zero = jnp.zeros((PAIR, PAIR), BF16)
    avg2 = jnp.concatenate([avg, avg], axis=0)
    stat2 = jnp.concatenate([jnp.concatenate([avg, zero], axis=1), jnp.concatenate([zero, ones], axis=1)], axis=0)
    return pl.pallas_call(
        _rwkv_kernel,
        grid=(bsz, seq // tb),
        in_specs=[tok] * 7 + [const((1, d))] * 3
        + [const((PAIR, PAIR)), const((2 * PAIR, PAIR)), const((2 * PAIR, 2 * PAIR))],
        out_specs=tok,
        out_shape=jax.ShapeDtypeStruct((bsz, seq, d), BF16),
        scratch_shapes=[pltpu.VMEM((d // PAIR, PAIR, PAIR), F32)] + slot + slot,
        compiler_params=pltpu.CompilerParams(dimension_semantics=("parallel", "arbitrary"),
                                             vmem_limit_bytes=VMEM_LIMIT),
    )(r, k, v, ld, a, b, g, r_k.reshape(1, d), lnx_w.reshape(1, d), lnx_b.reshape(1, d),
      _head_block_diag(PAIR, F32), avg2, stat2)


def _s5_prep_kernel(are_ref, aim_ref, ldt_ref, btre_ref, btim_ref, cre_ref, cim_ref, dsk_ref,
                    qt_out, p_out, t_out, lam_out):
    a_re = are_ref[...]
    a_im = aim_ref[...]
    dt = jnp.exp(ldt_ref[...])
    nd = 24
    dpow = lax.broadcasted_iota(jnp.int32, (nd, S5_STATE), 0).astype(F32)
    mag = jnp.exp(dpow * (dt * a_re))
    ang = dpow * (dt * a_im)
    e_re = mag * jnp.cos(ang)
    e_im = mag * jnp.sin(ang)
    lam_re = e_re[1:2, :]
    lam_im = e_im[1:2, :]
    den = a_re * a_re + a_im * a_im
    pp = lam_re - 1.0
    qq = lam_im
    coef_re = (pp * a_re + qq * a_im) / den
    coef_im = (qq * a_re - pp * a_im) / den
    bt_re = btre_ref[...]
    bt_im = btim_ref[...]
    bb_re = coef_re * bt_re - coef_im * bt_im
    bb_im = coef_re * bt_im + coef_im * bt_re
    c_re = cre_ref[...]
    c_im = cim_ref[...]
    c_lam = []
    for d in range(S5_L + 1):
        er = e_re[d:d + 1, :]
        ei = e_im[d:d + 1, :]
        c_lam.append(jnp.concatenate([c_re * er - c_im * ei, -(c_re * ei + c_im * er)], axis=1))
    qt_out[...] = jnp.concatenate(c_lam[1:], axis=0)
    for j in range(S5_L):
        d = S5_L - 1 - j
        er = e_re[d:d + 1, :]
        ei = e_im[d:d + 1, :]
        p_out[j * S5_CH:(j + 1) * S5_CH, :] = jnp.concatenate(
            [er * bb_re - ei * bb_im, er * bb_im + ei * bb_re], axis=1)
    y0 = jnp.concatenate([bb_re, bb_im], axis=1)
    for j in range(S5_L):
        lagged = jnp.concatenate([jnp.zeros_like(c_lam[0])] * j + c_lam[:S5_L - j], axis=0)
        t_out[j * S5_CH:(j + 1) * S5_CH, :] = lax.dot_general(
            y0, lagged, (((1,), (1,)), ((), ())), preferred_element_type=F32, precision=lax.Precision.HIGHEST)
    gw = S5_L * S5_CH
    on_diag = lax.broadcasted_iota(jnp.int32, (gw, gw), 0) == lax.broadcasted_iota(jnp.int32, (gw, gw), 1)
    t_out[...] = t_out[...] + jnp.where(on_diag, dsk_ref[...], 0.0)
    lam_out[...] = jnp.concatenate([e_re[S5_L:S5_L + 1, :], e_im[S5_L:S5_L + 1, :]], axis=1)


def _s5_prep(a_re, a_im, log_dt, b_re, b_im, c_re, c_im, d_skip):
    g, p = a_re.shape
    per_g = lambda shape: pl.BlockSpec((None,) + shape, lambda i: (i,) + (0,) * len(shape))
    gw = S5_L * S5_CH
    return pl.pallas_call(
        _s5_prep_kernel,
        grid=(g,),
        in_specs=[per_g((1, p)), per_g((1, p)), per_g((1, 1)), per_g((S5_CH, p)), per_g((S5_CH, p)),
                  per_g((S5_CH, p)), per_g((S5_CH, p)), per_g((1, gw))],
        out_specs=[per_g((gw, 2 * p)), per_g((gw, 2 * p)), per_g((gw, gw)), per_g((1, 2 * p))],
        out_shape=[jax.ShapeDtypeStruct((g, gw, 2 * p), F32),
                   jax.ShapeDtypeStruct((g, gw, 2 * p), F32),
                   jax.ShapeDtypeStruct((g, gw, gw), F32),
                   jax.ShapeDtypeStruct((g, 1, 2 * p), F32)],
        compiler_params=pltpu.CompilerParams(dimension_semantics=("arbitrary",)),
    )(a_re.reshape(g, 1, p), a_im.reshape(g, 1, p), log_dt.reshape(g, 1, 1),
      jnp.swapaxes(b_re, 1, 2), jnp.swapaxes(b_im, 1, 2), c_re, c_im,
      jnp.tile(d_skip, (1, S5_L)).reshape(g, 1, gw))


def _s5_weights_kernel(p_ref, t_ref, qt_ref, perm_ref, bp_out, bt_out, bq_out, bd_ref):
    perm = perm_ref[...]
    gw = p_ref.shape[1]

    def block_diag(blocks):
        bd_ref[...] = jnp.zeros_like(bd_ref)
        for h, blk in enumerate(blocks):
            bd_ref[h * gw:(h + 1) * gw, h * gw:(h + 1) * gw] = blk.astype(BF16)
        return bd_ref[...]

    n = p_ref.shape[0]
    bp_out[...] = _dot(perm, block_diag([p_ref[h] for h in range(n)])).astype(BF16)
    rows_ok = _dot(perm, block_diag([t_ref[h] for h in range(n)])).astype(BF16)
    bt_out[...] = _dot_nt(rows_ok, perm).astype(BF16)
    bq_out[...] = _dot_nt(block_diag([qt_ref[h].T for h in range(n)]), perm).astype(BF16)


def _s5_weights(p_all, t_all, qt_all):
    g, gw, sw = p_all.shape
    dst = jnp.arange(S5_W)
    j, h, a = dst // LANES, (dst % LANES) // S5_CH, dst % S5_CH
    src = h * gw + j * S5_CH + a
    perm = (src[:, None] == jnp.arange(S5_W)[None, :]).astype(BF16)
    grp = lambda n: pl.BlockSpec((S5_QG, gw, n), lambda q: (q, 0, 0))
    big = lambda r, c: pl.BlockSpec((None, r, c), lambda q: (q, 0, 0))
    return pl.pallas_call(
        _s5_weights_kernel,
        grid=(S5_NQ,),
        in_specs=[grp(sw), grp(gw), grp(sw), pl.BlockSpec((S5_W, S5_W), lambda q: (0, 0))],
        out_specs=[big(S5_W, S5_SW), big(S5_W, S5_W), big(S5_SW, S5_W)],
        out_shape=[jax.ShapeDtypeStruct((S5_NQ, S5_W, S5_SW), BF16),
                   jax.ShapeDtypeStruct((S5_NQ, S5_W, S5_W), BF16),
                   jax.ShapeDtypeStruct((S5_NQ, S5_SW, S5_W), BF16)],
        scratch_shapes=[pltpu.VMEM((S5_W, S5_W), BF16)],
        compiler_params=pltpu.CompilerParams(dimension_semantics=("arbitrary",),
                                             vmem_limit_bytes=VMEM_LIMIT),
    )(p_all, t_all, qt_all, perm)


def _s5_kernel(u_ref, bp_ref, bt_ref, bq_ref, lr_ref, li_ref, y_ref, z_ref, s_ref):
    @pl.when(pl.program_id(0) == 0)
    def _():
        s_ref[...] = jnp.zeros_like(s_ref)

    nb, ct, _ = u_ref.shape
    rows = nb * ct
    kb = S5_SW // LANES

    def cols(q):
        return [slice(tl * D_S5 + q * LANES, tl * D_S5 + (q + 1) * LANES) for tl in range(S5_L)]

    def x_of(q):
        return jnp.concatenate([u_ref[:, :, cs].reshape(rows, LANES) for cs in cols(q)], axis=1)

    def srow(q, k):
        return pl.ds(q * kb + k, rows, stride=S5_SROWS)

    for q in range(S5_NQ):
        z = _dot(x_of(q), bp_ref[q])
        for k in range(kb):
            z_ref[srow(q, k), :] = z[:, k * LANES:(k + 1) * LANES]

    lr = lr_ref[...]
    li = li_ref[...]

    def step(c, carry):
        s, s_sw = carry
        at = [pl.ds(pl.multiple_of((b * ct + c) * S5_SROWS, S5_SROWS), S5_SROWS) for b in range(nb)]
        z = jnp.concatenate([z_ref[at[b], :] for b in range(nb)], axis=0)
        for b in range(nb):
            z_ref[at[b], :] = s[b * S5_SROWS:(b + 1) * S5_SROWS]
        z_sw = pltpu.roll(z, S5_STATE, axis=1)
        return lr * s + li * s_sw + z, lr * s_sw - li * s + z_sw

    s0 = s_ref[...]
    s_end, _ = lax.fori_loop(0, ct, step, (s0, pltpu.roll(s0, S5_STATE, axis=1)), unroll=2)
    s_ref[...] = s_end

    for q in range(S5_NQ):
        start = jnp.concatenate([z_ref[srow(q, k), :] for k in range(kb)], axis=1).astype(BF16)
        y = (_dot(x_of(q), bt_ref[q]) + _dot(start, bq_ref[q])).astype(y_ref.dtype)
        for tl, cs in enumerate(cols(q)):
            y_ref[:, :, cs] = y[:, tl * LANES:(tl + 1) * LANES].reshape(nb, ct, LANES)


def _s5_core(u2, a_re, a_im, log_dt, b_re, b_im, c_re, c_im, d_skip, ct):
    bsz, nck, _ = u2.shape
    p, ll, nq = S5_STATE, S5_L, S5_NQ
    qt_all, p_all, t_all, lam = _s5_prep(a_re, a_im, log_dt, b_re, b_im, c_re, c_im, d_skip)
    big_p, big_t, big_q = _s5_weights(p_all, t_all, qt_all)
    lam_re = lam[:, 0, :p]
    lam_im = lam[:, 0, p:]
    lr = jnp.tile(jnp.concatenate([lam_re, lam_re], axis=1), (bsz, 1))
    li = jnp.tile(jnp.concatenate([-lam_im, lam_im], axis=1), (bsz, 1))

    once = dict(pipeline_mode=pl.Buffered(1))
    const = lambda shape: pl.BlockSpec(shape, lambda i: (0,) * len(shape), **once)
    tok = pl.BlockSpec((bsz, ct, ll * D_S5), lambda i: (0, i, 0))
    y2 = pl.pallas_call(
        _s5_kernel,
        grid=(nck // ct,),
        in_specs=[tok, const((nq, S5_W, S5_SW)), const((nq, S5_W, S5_W)), const((nq, S5_SW, S5_W)),
                  const((bsz * S5_SROWS, LANES)), const((bsz * S5_SROWS, LANES))],
        out_specs=tok,
        out_shape=jax.ShapeDtypeStruct((bsz, nck, ll * D_S5), BF16),
        scratch_shapes=[pltpu.VMEM((bsz * ct * S5_SROWS, LANES), F32),
                        pltpu.VMEM((bsz * S5_SROWS, LANES), F32)],
        compiler_params=pltpu.CompilerParams(dimension_semantics=("arbitrary",),
                                             vmem_limit_bytes=VMEM_LIMIT),
    )(u2, big_p, big_t, big_q, lr, li)
    return y2


def _tail_kernel(x_ref, yr_ref, ys_ref, gm_ref, shf_ref, scf_ref, gf_ref, wglu_ref, bglu_ref, gain_ref,
                 wo_ref, wg_ref, wu_ref, wd_ref, fg_ref, o_ref, ys_scr, act_scr, *, ff_tile):
    tm = x_ref.shape[0]
    for tl in range(S5_L):
        for k in range(S5_NQ):
            lo = tl * D_S5 + k * LANES
            ys_scr[k, pl.ds(tl, tm // S5_L, stride=S5_L), :] = ys_ref[:, lo:lo + LANES].astype(F32)
    ys = jnp.concatenate([ys_scr[k] for k in range(S5_NQ)], axis=1)
    zz = 0.5 * ys * (1.0 + jnp.tanh(math.sqrt(2.0 / math.pi) * (ys + 0.044715 * (ys * ys * ys))))
    gl = zz * _sigmoid(_dot(zz.astype(BF16), wglu_ref[...]) + bglu_ref[...])
    gl = gl * lax.rsqrt(jnp.mean(gl * gl, axis=-1, keepdims=True) + NORM_EPS) * gain_ref[...]
    mix = _dot(yr_ref[...].astype(BF16), wo_ref[:D_RWKV, :]) + _dot(gl.astype(BF16), wo_ref[D_RWKV:, :])
    x1 = x_ref[...] + gm_ref[...] * mix
    h = x1 * lax.rsqrt(jnp.mean(x1 * x1, axis=-1, keepdims=True) + NORM_EPS)
    h = (h * (1.0 + scf_ref[...]) + shf_ref[...]).astype(BF16)
    for j in range(D_FF // ff_tile):
        cs = slice(j * ff_tile, (j + 1) * ff_tile)
        gate = _dot(h, wg_ref[:, cs])
        up = _dot(h, wu_ref[:, cs])
        act_scr[:, cs] = (gate * _sigmoid(gate) * up).astype(BF16)
    x2 = x1 + gf_ref[...] * _dot(act_scr[...], wd_ref[...])
    o_ref[...] = x2 * lax.rsqrt(jnp.mean(x2 * x2, axis=-1, keepdims=True) + NORM_EPS) * fg_ref[...]


def _tail(x, y_rwkv, y_s5, g_m, sh_f, sc_f, g_f, w_glu, b_glu, gain, w_out, w_gate, w_up, w_down,
          final_gain, tm, ff_tile):
    bsz, seq, d = x.shape
    once = dict(pipeline_mode=pl.Buffered(1))
    const = lambda shape: pl.BlockSpec(shape, lambda b, t: (0,) * len(shape), **once)
    tok = lambda n: pl.BlockSpec((None, tm, n), lambda b, t: (b, t, 0))
    per_b = pl.BlockSpec((None, 1, d), lambda b, t: (b, 0, 0))
    b3 = lambda a: a.reshape(bsz, 1, d)
    return pl.pallas_call(
        functools.partial(_tail_kernel, ff_tile=ff_tile),
        grid=(bsz, seq // tm),
        in_specs=[tok(d), tok(D_RWKV), pl.BlockSpec((None, tm // S5_L, S5_L * D_S5), lambda b, t: (b, t, 0)),
                  per_b, per_b, per_b, per_b,
                  const((D_S5, D_S5)), const((1, D_S5)), const((1, D_S5)),
                  const((D_RWKV + D_S5, d)), const((d, D_FF)), const((d, D_FF)), const((D_FF, d)),
                  const((1, d))],
        out_specs=tok(d),
        out_shape=jax.ShapeDtypeStruct((bsz, seq, d), F32),
        scratch_shapes=[pltpu.VMEM((S5_NQ, tm, LANES), F32), pltpu.VMEM((tm, D_FF), BF16)],
        compiler_params=pltpu.CompilerParams(dimension_semantics=("parallel", "parallel"),
                                             vmem_limit_bytes=VMEM_LIMIT),
    )(x, y_rwkv, y_s5, b3(g_m), b3(sh_f), b3(sc_f), b3(g_f), w_glu.astype(BF16), b_glu.reshape(1, -1),
      gain.reshape(1, -1), w_out.astype(BF16), w_gate.astype(BF16), w_up.astype(BF16),
      w_down.astype(BF16), final_gain.reshape(1, d))


def kernel(x, c, w_ada, b_ada, w_in, mu_shift, rw_w0, rw_w2, rw_a0, rw_a2, rw_g2, rw_k_k, rw_k_a, rw_r_k,
           rw_lnx_w, rw_lnx_b, s5_a_re, s5_a_im, s5_log_dt, s5_b_re, s5_b_im, s5_c_re, s5_c_im, s5_d,
           s5_w_glu, s5_b_glu, s5_gain, w_out, ffn_w_gate, ffn_w_up, ffn_w_down, final_gain):
    assert w_ada.shape[0] == 1, "single-layer trunk"
    seq = x.shape[1]
    ada = _ada(c, w_ada[0], b_ada[0])
    sh_m, sc_m, g_m, sh_f, sc_f, g_f = jnp.split(ada, 6, axis=-1)
    r, k, v, ld, a, b, g, u = _inproj(x, sh_m, sc_m, w_in[0], mu_shift[0], rw_w0[0], rw_w2[0], rw_a0[0],
                                      rw_a2[0], rw_g2[0], rw_k_k[0], rw_k_a[0], tm=min(512, seq))
    y_rwkv = _rwkv(r, k, v, ld, a, b, g, rw_r_k[0], rw_lnx_w[0], rw_lnx_b[0], tb=min(2048, seq))
    y_s5 = _s5_core(u, s5_a_re[0], s5_a_im[0], s5_log_dt[0], s5_b_re[0], s5_b_im[0], s5_c_re[0],
                    s5_c_im[0], s5_d[0], ct=min(128, seq // S5_L))
    return _tail(x, y_rwkv, y_s5, g_m, sh_f, sc_f, g_f, s5_w_glu[0], s5_b_glu[0], s5_gain[0], w_out[0],
                 ffn_w_gate[0], ffn_w_up[0], ffn_w_down[0], final_gain, tm=min(512, seq), ff_tile=256)
```

```python
import functools
import math

import jax
import jax.numpy as jnp
from jax import lax
from jax.experimental import pallas as pl
from jax.experimental.pallas import tpu as pltpu

F32 = jnp.float32
BF16 = jnp.bfloat16

D_MODEL = 1024
D_RWKV = 512
D_S5 = 512
HEAD = 64
LORA_W = 64
LORA_A = 64
LORA_G = 128
S5_CH = 16
S5_GROUPS = 32
S5_STATE = 64
D_FF = 2816
D_SHIFT = 3 * D_RWKV + LORA_W + LORA_A + LORA_G
D_IN = D_SHIFT + D_S5
NORM_EPS = 1e-6
LNX_EPS = 64e-5

CHUNK = 64
PAIR = 2 * HEAD
INPROJ_SUB = 128
UNIT_CHUNKS = 4
S5_L = 8
LANES = 128
S5_QG = LANES // S5_CH
S5_NQ = D_S5 // LANES
S5_W = S5_L * LANES
S5_SW = S5_QG * 2 * S5_STATE
S5_SROWS = S5_NQ * S5_SW // LANES
VMEM_LIMIT = 56 * 1024 * 1024


def _split_bf16(x, n):
    parts = []
    rem = x
    for i in range(n):
        p = rem.astype(BF16)
        parts.append(p)
        if i + 1 < n:
            rem = rem - p.astype(F32)
    return parts


def _dot(a, b):
    return jnp.dot(a, b, preferred_element_type=F32)


def _dot_nt(a, b):
    return lax.dot_general(a, b, (((1,), (1,)), ((), ())), preferred_element_type=F32)


def _dot_split_lhs(x, rhs_bf16, n):
    acc = None
    for p in _split_bf16(x, n):
        d = _dot(p, rhs_bf16)
        acc = d if acc is None else acc + d
    return acc


def _sigmoid(x):
    return 1.0 / (1.0 + jnp.exp(-x))


def _ada_kernel(c_ref, w_ref, b_ref, o_ref):
    c = c_ref[...]
    act = c * _sigmoid(c)
    o_ref[...] = jnp.dot(act, w_ref[...], preferred_element_type=F32,
                         precision=lax.Precision.HIGHEST) + b_ref[...]


def _ada(c, w_ada, b_ada):
    bsz, d = c.shape
    rows = 8
    c_pad = jnp.zeros((rows, d), F32).at[:bsz].set(c)
    n_out = w_ada.shape[1]
    out = pl.pallas_call(
        _ada_kernel,
        grid=(n_out // d,),
        in_specs=[
            pl.BlockSpec((rows, d), lambda j: (0, 0)),
            pl.BlockSpec((d, d), lambda j: (0, j)),
            pl.BlockSpec((1, d), lambda j: (0, j)),
        ],
        out_specs=pl.BlockSpec((rows, d), lambda j: (0, j)),
        out_shape=jax.ShapeDtypeStruct((rows, n_out), F32),
        compiler_params=pltpu.CompilerParams(dimension_semantics=("arbitrary",)),
    )(c_pad, w_ada, b_ada.reshape(1, n_out))
    return out[:bsz]


def _inproj_kernel(x_ref, sh_ref, sc_ref, win_ref, mu_ref, w0_ref, w2_ref, a0_ref, a2_ref, g2_ref,
                   kk_ref, ka_ref, bd_ref,
                   r_out, k_out, v_out, ld_out, a_out, b_out, g_out, u_out, carry_ref, u_scr):
    @pl.when(pl.program_id(1) == 0)
    def _():
        carry_ref[...] = jnp.zeros_like(carry_ref)

    sub = INPROJ_SUB
    prev_last = carry_ref[...]
    for s in range(x_ref.shape[0] // sub):
        rows = slice(s * sub, (s + 1) * sub)
        crow = slice(s * (sub // S5_L), (s + 1) * (sub // S5_L))
        prev_last = _inproj_rows(x_ref[rows, :], prev_last, rows, crow, sh_ref, sc_ref, win_ref, mu_ref,
                                 w0_ref, w2_ref, a0_ref, a2_ref, g2_ref, kk_ref, ka_ref, bd_ref,
                                 r_out, k_out, v_out, ld_out, a_out, b_out, g_out, u_out, u_scr)
    carry_ref[...] = prev_last


def _inproj_rows(x, prev_last, rows, crow, sh_ref, sc_ref, win_ref, mu_ref, w0_ref, w2_ref, a0_ref, a2_ref,
                 g2_ref, kk_ref, ka_ref, bd_ref, r_out, k_out, v_out, ld_out, a_out, b_out, g_out, u_out, u_scr):
    tm = x.shape[0]
    ms = jnp.mean(x * x, axis=-1, keepdims=True)
    h = x * lax.rsqrt(ms + NORM_EPS)
    h = h * (1.0 + sc_ref[...]) + sh_ref[...]
    proj = _dot(h.astype(BF16), win_ref[...])
    z = proj[:, :D_SHIFT]
    for k in range(S5_NQ):
        u_scr[k, rows, :] = proj[:, D_SHIFT + k * LANES:D_SHIFT + (k + 1) * LANES]
    for tl in range(S5_L):
        for k in range(S5_NQ):
            lo = tl * D_S5 + k * LANES
            picked = u_scr[k, pl.ds(rows.start + tl, tm // S5_L, stride=S5_L), :]
            u_out[crow, lo:lo + LANES] = picked.astype(u_out.dtype)

    z_roll = pltpu.roll(z, 1, axis=0)
    row = lax.broadcasted_iota(jnp.int32, z.shape, 0)
    z_prev = jnp.where(row == 0, prev_last, z_roll)
    zz = z + mu_ref[...] * (z_prev - z)

    r = zz[:, 0:D_RWKV]
    k = zz[:, D_RWKV:2 * D_RWKV]
    v = zz[:, 2 * D_RWKV:3 * D_RWKV]
    o = 3 * D_RWKV
    w_lo = zz[:, o:o + LORA_W]
    a_lo = zz[:, o + LORA_W:o + LORA_W + LORA_A]
    g_lo = zz[:, o + LORA_W + LORA_A:o + LORA_W + LORA_A + LORA_G]

    wl = w0_ref[...] + _dot(jnp.tanh(w_lo).astype(BF16), w2_ref[...])
    ld_out[rows, :] = -math.exp(-0.5) * _sigmoid(wl)
    a = _sigmoid(a0_ref[...] + _dot(a_lo.astype(BF16), a2_ref[...]))
    g_out[rows, :] = _dot(_sigmoid(g_lo).astype(BF16), g2_ref[...]).astype(g_out.dtype)

    kk = k * kk_ref[...]
    n2 = _dot((kk * kk).astype(BF16), bd_ref[...])
    kk = kk / jnp.maximum(jnp.sqrt(n2), 1e-12)
    r_out[rows, :] = r.astype(r_out.dtype)
    k_out[rows, :] = (k * (1.0 + (a - 1.0) * ka_ref[...])).astype(k_out.dtype)
    v_out[rows, :] = v.astype(v_out.dtype)
    a_out[rows, :] = (-kk).astype(a_out.dtype)
    b_out[rows, :] = (kk * a).astype(b_out.dtype)
    return z[tm - 1:tm, :]


def _head_block_diag(n, dtype, value=1.0):
    i = jnp.arange(n) // HEAD
    return jnp.where(i[:, None] == i[None, :], value, 0.0).astype(dtype)


def _inproj(x, sh_m, sc_m, w_in, mu, w0, w2, a0, a2, g2, k_k, k_a, tm):
    bsz, seq, d = x.shape
    row = lambda a: a.reshape(1, -1)
    const = lambda shape: pl.BlockSpec(shape, lambda b, t: (0,) * len(shape), pipeline_mode=pl.Buffered(1))
    tok = lambda n: pl.BlockSpec((None, tm, n), lambda b, t: (b, t, 0))
    per_b = pl.BlockSpec((None, 1, d), lambda b, t: (b, 0, 0))
    out_sds = lambda dt: jax.ShapeDtypeStruct((bsz, seq, D_RWKV), dt)
    out_dtypes = [BF16, BF16, BF16, F32, BF16, BF16, BF16]
    return pl.pallas_call(
        _inproj_kernel,
        grid=(bsz, seq // tm),
        in_specs=[tok(d), per_b, per_b, const((d, D_IN)), const((1, D_SHIFT)),
                  const((1, D_RWKV)), const((LORA_W, D_RWKV)), const((1, D_RWKV)),
                  const((LORA_A, D_RWKV)), const((LORA_G, D_RWKV)), const((1, D_RWKV)),
                  const((1, D_RWKV)), const((D_RWKV, D_RWKV))],
        out_specs=[tok(D_RWKV)] * 7 + [pl.BlockSpec((None, tm // S5_L, S5_L * D_S5), lambda b, t: (b, t, 0))],
        out_shape=[out_sds(dt) for dt in out_dtypes]
        + [jax.ShapeDtypeStruct((bsz, seq // S5_L, S5_L * D_S5), BF16)],
        scratch_shapes=[pltpu.VMEM((1, D_SHIFT), F32), pltpu.VMEM((S5_NQ, tm, LANES), F32)],
        compiler_params=pltpu.CompilerParams(dimension_semantics=("parallel", "arbitrary"),
                                             vmem_limit_bytes=VMEM_LIMIT),
    )(x, sh_m.reshape(bsz, 1, d), sc_m.reshape(bsz, 1, d), w_in.astype(BF16), row(mu),
      row(w0), w2.astype(BF16), row(a0), a2.astype(BF16), g2.astype(BF16), row(k_k), row(k_a),
      _head_block_diag(D_RWKV, BF16))


def _per_head(yb, lo, hi):
    return jnp.concatenate([yb * lo, yb * hi], axis=0)


def _rwkv_prep(slabs, rk_rows, cst):
    row, strict, incl, bd, eye, m0, m1, m0w, m1w, eye_c, merge_masks = cst
    c = CHUNK
    r, k, v, ld, a, b = ([s[i] for s in slabs] for i in range(6))
    cum = ld
    sft = 1
    while sft < c:
        if sft < 8:
            cum = [x + jnp.where(row >= sft, pltpu.roll(x, sft, axis=0), 0.0) for x in cum]
        else:
            pad = jnp.zeros((sft, PAIR), F32)
            cum = [x + jnp.concatenate([pad, x[:c - sft]], axis=0) for x in cum]
        sft *= 2
    cl = [x[c - 1:c, :] for x in cum]
    w_to = [jnp.exp(x) for x in cum]
    w_inv = [jnp.exp(-x) for x in cum]
    w_prev = [jnp.exp(x - y) for x, y in zip(cum, ld)]
    w_rem = [jnp.exp(y - x) for x, y in zip(cum, cl)]
    w_all = [jnp.exp(y) for y in cl]
    rt = [x * w for x, w in zip(r, w_to)]
    kt = [x * w for x, w in zip(k, w_inv)]
    at = [x * w for x, w in zip(a, w_prev)]
    bt = [x * w for x, w in zip(b, w_inv)]
    bh = [x * w for x, w in zip(b, w_rem)]
    kh = [x * w for x, w in zip(k, w_rem)]
    lhs = [jnp.concatenate([x, y], axis=0).astype(BF16) for x, y in zip(at, rt)]
    rhs = [jnp.concatenate([_per_head(x.astype(BF16), m0, m1), _per_head(y.astype(BF16), m0, m1)], axis=0)
           for x, y in zip(bt, kt)]
    vb = [x.astype(BF16) for x in v]
    bk_t = [jnp.concatenate([x, y], axis=0).T.astype(BF16) for x, y in zip(bh, kh)]
    rkb = [(x * y * z).astype(BF16) for x, y, z in zip(r, k, rk_rows)]
    return list(zip(lhs, rhs, vb, at, rt, bk_t, w_all, rkb))


def _rwkv_chunk_maps(ops, cst):
    row, strict, incl, bd, eye, m0, m1, m0w, m1w, eye_c, merge_masks = cst
    c = CHUNK
    lhs, rhs, vb, at, rt, bk_t, w_all, _ = ([o[i] for o in ops] for i in range(8))
    per_head = _per_head
    a_all = [_dot_nt(x, y) for x, y in zip(lhs, rhs)]
    n_ab = [jnp.where(strict, x[:c, :PAIR], 0.0) for x in a_all]
    a_ak = [jnp.where(strict, x[:c, PAIR:], 0.0) for x in a_all]
    a_rb = [jnp.where(incl, x[c:, :PAIR], 0.0) for x in a_all]
    a_rk = [jnp.where(incl, x[c:, PAIR:], 0.0) for x in a_all]

    vs = [per_head(x, m0, m1) for x in vb]
    x0 = [jnp.concatenate([y, _dot(z.astype(BF16), w)], axis=1) for y, z, w in zip(at, a_ak, vs)]

    nb = [x.astype(BF16) for x in n_ab]
    t_inv = [eye_c + x * merge_masks[0] for x in n_ab]
    anchors = [None] * len(ops)
    for lvl, msk in enumerate(merge_masks[1:]):
        tb = [t.astype(BF16) for t in t_inv]
        tmp = [_dot(t, per_head(n * msk, m0, m1)) for t, n in zip(tb, nb)]
        t_inv = [t + _dot(m.astype(BF16), per_head(h, m0, m1)) for t, m, h in zip(t_inv, tmp, tb)]
        for n in range(len(ops)):
            if lvl == (n * (len(merge_masks) - 1)) // len(ops):
                anchors[n] = tmp[n]
    x1 = [_dot(t.astype(BF16), per_head(y.astype(BF16), m0w, m1w)) for t, y in zip(t_inv, x0)]
    x1b = [x.astype(BF16) for x in x1]
    rx = [_dot(z.astype(BF16), per_head(y, m0w, m1w)) for z, y in zip(a_rb, x1b)]
    r_hat = [x + y[:, :PAIR] for x, y in zip(rt, rx)]
    y_hat = [y[:, PAIR:] + _dot(z.astype(BF16), w) for y, z, w in zip(rx, a_rk, vs)]
    zero = jnp.zeros((c, PAIR), BF16)
    gm = [_dot(x, jnp.concatenate([y, jnp.concatenate([zero, z], axis=1)], axis=0))
          for x, y, z in zip(bk_t, x1b, vb)]
    m_mat = [eye * w + x[:, :PAIR] * bd for w, x in zip(w_all, gm)]
    s_hat = [x[:, PAIR:] * bd for x in gm]
    return list(zip(r_hat, y_hat, m_mat, s_hat)), anchors


def _rwkv_kernel(r_ref, k_ref, v_ref, ld_ref, a_ref, b_ref, g_ref, rk_ref, lw_ref, lb_ref,
                 bd_ref, avg2_ref, stat2_ref, y_ref, s_ref, *prep_refs):
    t = pl.program_id(1)

    @pl.when(t == 0)
    def _():
        s_ref[...] = jnp.zeros_like(s_ref)

    c = CHUNK
    lane = lax.broadcasted_iota(jnp.int32, (1, PAIR), 1)
    m0 = (lane < HEAD).astype(F32).astype(BF16)
    m1 = (lane >= HEAD).astype(F32).astype(BF16)
    lane_w = lax.broadcasted_iota(jnp.int32, (1, 2 * PAIR), 1) % PAIR
    m0w = (lane_w < HEAD).astype(F32).astype(BF16)
    m1w = (lane_w >= HEAD).astype(F32).astype(BF16)
    row = lax.broadcasted_iota(jnp.int32, (c, PAIR), 0)
    col = lax.broadcasted_iota(jnp.int32, (c, PAIR), 1) % c
    strict = col < row
    incl = col <= row
    bd = bd_ref[...]
    ri = lax.broadcasted_iota(jnp.int32, (PAIR, PAIR), 0)
    ci = lax.broadcasted_iota(jnp.int32, (PAIR, PAIR), 1)
    eye = (ri == ci).astype(F32)
    eye_c = (col == row).astype(F32)
    merge_masks = []
    sz = 1
    while sz < c:
        msk = (row // (2 * sz) == col // (2 * sz)) & (row % (2 * sz) >= sz) & (col % (2 * sz) < sz)
        merge_masks.append(msk.astype(F32) if sz == 1 else msk.astype(F32).astype(BF16))
        sz *= 2
    cst = (row, strict, incl, bd, eye, m0, m1, m0w, m1w, eye_c, merge_masks)

    n_pairs = D_RWKV // PAIR
    span = UNIT_CHUNKS * c
    units = [(ch, p) for ch in range(UNIT_CHUNKS) for p in range(n_pairs)]
    lanes = [slice(p * PAIR, (p + 1) * PAIR) for _, p in units]
    n_blocks = r_ref.shape[0] // span
    slot_a, slot_b = prep_refs[:len(prep_refs) // 2], prep_refs[len(prep_refs) // 2:]

    def unit_rows(blk):
        base = pl.multiple_of(blk * span, span)
        return [pl.ds(base + ch * c, c) for ch, _ in units]

    def exact_zero(x):
        bits = pltpu.bitcast(x, jnp.uint32)
        half = jnp.uint32(16)
        return pltpu.bitcast(lax.shift_right_logical(lax.shift_right_logical(bits, half), half), F32)

    def prep_block(blk, slot, after=None):
        slabs = [tuple(ref[rs, ls].astype(F32) for ref in (r_ref, k_ref, v_ref, ld_ref, a_ref, b_ref))
                 for rs, ls in zip(unit_rows(blk), lanes)]
        if after is not None:
            slabs = [s[:3] + (s[3] + exact_zero(x),) + s[4:] for s, x in zip(slabs, after)]
        ops = _rwkv_prep(slabs, [rk_ref[:, ls] for ls in lanes], cst)
        for n, op in enumerate(ops):
            for ref, val in zip(slot, op):
                ref[n] = val

    def main_block(blk, slot):
        ops = [tuple(ref[n] for ref in slot) for n in range(len(units))]
        maps, anchors = _rwkv_chunk_maps(ops, cst)

        state = [s_ref[p] for p in range(n_pairs)]
        ys = []
        for (ch, p), (r_hat, y_hat, m_mat, s_hat) in zip(units, maps):
            sb = state[p].astype(BF16)
            ym = _dot(jnp.concatenate([r_hat, m_mat], axis=0).astype(BF16), sb)
            ys.append(ym[:c] + y_hat)
            state[p] = ym[c:] + s_hat
        for p in range(n_pairs):
            s_ref[p] = state[p]

        y_all = jnp.concatenate(ys, axis=0)
        dlt = y_all - _dot(jnp.concatenate(_split_bf16(y_all, 2), axis=1), avg2_ref[...])
        rkb = jnp.concatenate([op[7] for op in ops], axis=0)
        stats = _dot(jnp.concatenate([(dlt * dlt).astype(BF16), rkb], axis=1), stat2_ref[...])
        yn = dlt * lax.rsqrt(stats[:, :PAIR] + LNX_EPS)
        bonus = stats[:, PAIR:]
        for n, (op, rs, ls) in enumerate(zip(ops, unit_rows(blk), lanes)):
            un = slice(n * c, (n + 1) * c)
            y = yn[un] * lw_ref[:, ls] + lb_ref[:, ls] + bonus[un] * op[2].astype(F32)
            y_ref[rs, ls] = (y * g_ref[rs, ls].astype(F32)).astype(y_ref.dtype)
        return anchors

    prep_block(0, slot_a)

    def body(j, carry):
        done = main_block(2 * j, slot_a)
        prep_block(2 * j + 1, slot_b, after=done)
        done = main_block(2 * j + 1, slot_b)
        prep_block(jnp.minimum(2 * j + 2, n_blocks - 1), slot_a, after=done)
        return carry

    lax.fori_loop(0, n_blocks // 2, body, 0)


def _rwkv(r, k, v, ld, a, b, g, r_k, lnx_w, lnx_b, tb):
    bsz, seq, d = r.shape
    tok = pl.BlockSpec((None, tb, d), lambda i, t: (i, t, 0))
    const = lambda shape: pl.BlockSpec(shape, lambda i, t: (0,) * len(shape))
    assert (tb // (UNIT_CHUNKS * CHUNK)) % 2 == 0, "the block pipeline is written for an even block count"
    c, nu = CHUNK, UNIT_CHUNKS * (d // PAIR)
    slot = [pltpu.VMEM((nu, 2 * c, PAIR), BF16), pltpu.VMEM((nu, 4 * c, PAIR), BF16),
            pltpu.VMEM((nu, c, PAIR), BF16), pltpu.VMEM((nu, c, PAIR), F32), pltpu.VMEM((nu, c, PAIR), F32),
            pltpu.VMEM((nu, PAIR, 2 * c), BF16), pltpu.VMEM((nu, 1, PAIR), F32), pltpu.VMEM((nu, c, PAIR), BF16)]
    avg = _head_block_diag(PAIR, BF16, 1.0 / HEAD)
    ones = _head_block_diag(PAIR, BF16)
    zero = jnp.zeros((PAIR, PAIR), BF16)
    avg2 = jnp.concatenate([avg, avg], axis=0)
    stat2 = jnp.concatenate([jnp.concatenate([avg, zero], axis=1), jnp.concatenate([zero, ones], axis=1)], axis=0)
    return pl.pallas_call(
        _rwkv_kernel,
        grid=(bsz, seq // tb),
        in_specs=[tok] * 7 + [const((1, d))] * 3
        + [const((PAIR, PAIR)), const((2 * PAIR, PAIR)), const((2 * PAIR, 2 * PAIR))],
        out_specs=tok,
        out_shape=jax.ShapeDtypeStruct((bsz, seq, d), BF16),
        scratch_shapes=[pltpu.VMEM((d // PAIR, PAIR, PAIR), F32)] + slot + slot,
        compiler_params=pltpu.CompilerParams(dimension_semantics=("parallel", "arbitrary"),
                                             vmem_limit_bytes=VMEM_LIMIT),
    )(r, k, v, ld, a, b, g, r_k.reshape(1, d), lnx_w.reshape(1, d), lnx_b.reshape(1, d),
      _head_block_diag(PAIR, F32), avg2, stat2)


def _s5_prep_kernel(are_ref, aim_ref, ldt_ref, btre_ref, btim_ref, cre_ref, cim_ref, dsk_ref,
                    qt_out, p_out, t_out, lam_out):
    for gi in range(are_ref.shape[0]):
        _s5_prep_group(are_ref[gi], aim_ref[gi], ldt_ref[gi], btre_ref[gi], btim_ref[gi], cre_ref[gi],
                       cim_ref[gi], dsk_ref[gi], qt_out.at[gi], p_out.at[gi], t_out.at[gi], lam_out.at[gi])


def _s5_prep_group(a_re, a_im, log_dt, bt_re, bt_im, c_re, c_im, d_tile, qt_out, p_out, t_out, lam_out):
    dt = jnp.exp(log_dt)
    nd = 24
    dpow = lax.broadcasted_iota(jnp.int32, (nd, S5_STATE), 0).astype(F32)
    mag = jnp.exp(dpow * (dt * a_re))
    ang = dpow * (dt * a_im)
    e_re = mag * jnp.cos(ang)
    e_im = mag * jnp.sin(ang)
    lam_re = e_re[1:2, :]
    lam_im = e_im[1:2, :]
    den = a_re * a_re + a_im * a_im
    pp = lam_re - 1.0
    qq = lam_im
    coef_re = (pp * a_re + qq * a_im) / den
    coef_im = (qq * a_re - pp * a_im) / den
    bb_re = coef_re * bt_re - coef_im * bt_im
    bb_im = coef_re * bt_im + coef_im * bt_re
    c_lam = []
    for d in range(S5_L + 1):
        er = e_re[d:d + 1, :]
        ei = e_im[d:d + 1, :]
        c_lam.append(jnp.concatenate([c_re * er - c_im * ei, -(c_re * ei + c_im * er)], axis=1))
    qt_out[...] = jnp.concatenate(c_lam[1:], axis=0)
    for j in range(S5_L):
        d = S5_L - 1 - j
        er = e_re[d:d + 1, :]
        ei = e_im[d:d + 1, :]
        p_out[j * S5_CH:(j + 1) * S5_CH, :] = jnp.concatenate(
            [er * bb_re - ei * bb_im, er * bb_im + ei * bb_re], axis=1)
    y0 = jnp.concatenate([bb_re, bb_im], axis=1)
    t_rows = []
    for j in range(S5_L):
        lagged = jnp.concatenate([jnp.zeros_like(c_lam[0])] * j + c_lam[:S5_L - j], axis=0)
        t_rows.append(lax.dot_general(y0, lagged, (((1,), (1,)), ((), ())), preferred_element_type=F32,
                                      precision=lax.Precision.HIGHEST))
    gw = S5_L * S5_CH
    on_diag = lax.broadcasted_iota(jnp.int32, (gw, gw), 0) == lax.broadcasted_iota(jnp.int32, (gw, gw), 1)
    t_out[...] = jnp.concatenate(t_rows, axis=0) + jnp.where(on_diag, d_tile, 0.0)
    lam_out[...] = jnp.concatenate([e_re[S5_L:S5_L + 1, :], e_im[S5_L:S5_L + 1, :]], axis=1)


def _s5_prep(a_re, a_im, log_dt, b_re, b_im, c_re, c_im, d_skip):
    g, p = a_re.shape
    per_g = lambda shape: pl.BlockSpec((S5_QG,) + shape, lambda i: (i,) + (0,) * len(shape))
    gw = S5_L * S5_CH
    return pl.pallas_call(
        _s5_prep_kernel,
        grid=(g // S5_QG,),
        in_specs=[per_g((1, p)), per_g((1, p)), per_g((1, 1)), per_g((S5_CH, p)), per_g((S5_CH, p)),
                  per_g((S5_CH, p)), per_g((S5_CH, p)), per_g((1, gw))],
        out_specs=[per_g((gw, 2 * p)), per_g((gw, 2 * p)), per_g((gw, gw)), per_g((1, 2 * p))],
        out_shape=[jax.ShapeDtypeStruct((g, gw, 2 * p), F32),
                   jax.ShapeDtypeStruct((g, gw, 2 * p), F32),
                   jax.ShapeDtypeStruct((g, gw, gw), F32),
                   jax.ShapeDtypeStruct((g, 1, 2 * p), F32)],
        compiler_params=pltpu.CompilerParams(dimension_semantics=("arbitrary",)),
    )(a_re.reshape(g, 1, p), a_im.reshape(g, 1, p), log_dt.reshape(g, 1, 1),
      jnp.swapaxes(b_re, 1, 2), jnp.swapaxes(b_im, 1, 2), c_re, c_im,
      jnp.tile(d_skip, (1, S5_L)).reshape(g, 1, gw))


def _s5_weights_kernel(p_ref, t_ref, qt_ref, perm_ref, bp_out, bt_out, bq_out, bd_ref):
    perm = perm_ref[...]
    gw = p_ref.shape[1]

    def block_diag(blocks):
        bd_ref[...] = jnp.zeros_like(bd_ref)
        for h, blk in enumerate(blocks):
            bd_ref[h * gw:(h + 1) * gw, h * gw:(h + 1) * gw] = blk.astype(BF16)
        return bd_ref[...]

    n = p_ref.shape[0]
    bp_out[...] = _dot(perm, block_diag([p_ref[h] for h in range(n)])).astype(BF16)
    rows_ok = _dot(perm, block_diag([t_ref[h] for h in range(n)])).astype(BF16)
    bt_out[...] = _dot_nt(rows_ok, perm).astype(BF16)
    bq_out[...] = _dot_nt(block_diag([qt_ref[h].T for h in range(n)]), perm).astype(BF16)


def _s5_weights(p_all, t_all, qt_all):
    g, gw, sw = p_all.shape
    dst = jnp.arange(S5_W)
    j, h, a = dst // LANES, (dst % LANES) // S5_CH, dst % S5_CH
    src = h * gw + j * S5_CH + a
    perm = (src[:, None] == jnp.arange(S5_W)[None, :]).astype(BF16)
    grp = lambda n: pl.BlockSpec((S5_QG, gw, n), lambda q: (q, 0, 0))
    big = lambda r, c: pl.BlockSpec((None, r, c), lambda q: (q, 0, 0))
    return pl.pallas_call(
        _s5_weights_kernel,
        grid=(S5_NQ,),
        in_specs=[grp(sw), grp(gw), grp(sw), pl.BlockSpec((S5_W, S5_W), lambda q: (0, 0))],
        out_specs=[big(S5_W, S5_SW), big(S5_W, S5_W), big(S5_SW, S5_W)],
        out_shape=[jax.ShapeDtypeStruct((S5_NQ, S5_W, S5_SW), BF16),
                   jax.ShapeDtypeStruct((S5_NQ, S5_W, S5_W), BF16),
                   jax.ShapeDtypeStruct((S5_NQ, S5_SW, S5_W), BF16)],
        scratch_shapes=[pltpu.VMEM((S5_W, S5_W), BF16)],
        compiler_params=pltpu.CompilerParams(dimension_semantics=("arbitrary",),
                                             vmem_limit_bytes=VMEM_LIMIT),
    )(p_all, t_all, qt_all, perm)


def _s5_kernel(u_ref, bp_ref, bt_ref, bq_ref, lr_ref, li_ref, y_ref, z_ref, s_ref):
    @pl.when(pl.program_id(0) == 0)
    def _():
        s_ref[...] = jnp.zeros_like(s_ref)

    nb, ct, _ = u_ref.shape
    rows = nb * ct
    kb = S5_SW // LANES

    def cols(q):
        return [slice(tl * D_S5 + q * LANES, tl * D_S5 + (q + 1) * LANES) for tl in range(S5_L)]

    def x_of(q):
        return jnp.concatenate([u_ref[:, :, cs].reshape(rows, LANES) for cs in cols(q)], axis=1)

    def srow(q, k):
        return pl.ds(q * kb + k, rows, stride=S5_SROWS)

    for q in range(S5_NQ):
        z = _dot(x_of(q), bp_ref[q])
        for k in range(kb):
            z_ref[srow(q, k), :] = z[:, k * LANES:(k + 1) * LANES]

    lr = lr_ref[...]
    li = li_ref[...]

    def step(c, carry):
        s, s_sw = carry
        at = [pl.ds(pl.multiple_of((b * ct + c) * S5_SROWS, S5_SROWS), S5_SROWS) for b in range(nb)]
        z = jnp.concatenate([z_ref[at[b], :] for b in range(nb)], axis=0)
        for b in range(nb):
            z_ref[at[b], :] = s[b * S5_SROWS:(b + 1) * S5_SROWS]
        z_sw = pltpu.roll(z, S5_STATE, axis=1)
        return lr * s + li * s_sw + z, lr * s_sw - li * s + z_sw

    s0 = s_ref[...]
    s_end, _ = lax.fori_loop(0, ct, step, (s0, pltpu.roll(s0, S5_STATE, axis=1)), unroll=4)
    s_ref[...] = s_end

    for q in range(S5_NQ):
        start = jnp.concatenate([z_ref[srow(q, k), :] for k in range(kb)], axis=1).astype(BF16)
        y = (_dot(x_of(q), bt_ref[q]) + _dot(start, bq_ref[q])).astype(y_ref.dtype)
        for tl, cs in enumerate(cols(q)):
            y_ref[:, :, cs] = y[:, tl * LANES:(tl + 1) * LANES].reshape(nb, ct, LANES)


def _s5_core(u2, a_re, a_im, log_dt, b_re, b_im, c_re, c_im, d_skip, ct):
    bsz, nck, _ = u2.shape
    p, ll, nq = S5_STATE, S5_L, S5_NQ
    qt_all, p_all, t_all, lam = _s5_prep(a_re, a_im, log_dt, b_re, b_im, c_re, c_im, d_skip)
    big_p, big_t, big_q = _s5_weights(p_all, t_all, qt_all)
    lam_re = lam[:, 0, :p]
    lam_im = lam[:, 0, p:]
    lr = jnp.tile(jnp.concatenate([lam_re, lam_re], axis=1), (bsz, 1))
    li = jnp.tile(jnp.concatenate([-lam_im, lam_im], axis=1), (bsz, 1))

    once = dict(pipeline_mode=pl.Buffered(1))
    const = lambda shape: pl.BlockSpec(shape, lambda i: (0,) * len(shape), **once)
    tok = pl.BlockSpec((bsz, ct, ll * D_S5), lambda i: (0, i, 0))
    y2 = pl.pallas_call(
        _s5_kernel,
        grid=(nck // ct,),
        in_specs=[tok, const((nq, S5_W, S5_SW)), const((nq, S5_W, S5_W)), const((nq, S5_SW, S5_W)),
                  const((bsz * S5_SROWS, LANES)), const((bsz * S5_SROWS, LANES))],
        out_specs=tok,
        out_shape=jax.ShapeDtypeStruct((bsz, nck, ll * D_S5), BF16),
        scratch_shapes=[pltpu.VMEM((bsz * ct * S5_SROWS, LANES), F32),
                        pltpu.VMEM((bsz * S5_SROWS, LANES), F32)],
        compiler_params=pltpu.CompilerParams(dimension_semantics=("arbitrary",),
                                             vmem_limit_bytes=VMEM_LIMIT),
    )(u2, big_p, big_t, big_q, lr, li)
    return y2


def _tail_kernel(x_ref, yr_ref, ys_ref, gm_ref, shf_ref, scf_ref, gf_ref, wglu_ref, bglu_ref, gain_ref,
                 wo_ref, wg_ref, wu_ref, wd_ref, fg_ref, o_ref, ys_scr, act_scr, *, ff_tile):
    tm = x_ref.shape[0]
    for tl in range(S5_L):
        for k in range(S5_NQ):
            lo = tl * D_S5 + k * LANES
            ys_scr[k, pl.ds(tl, tm // S5_L, stride=S5_L), :] = ys_ref[:, lo:lo + LANES].astype(F32)
    ys = jnp.concatenate([ys_scr[k] for k in range(S5_NQ)], axis=1)
    zz = 0.5 * ys * (1.0 + jnp.tanh(math.sqrt(2.0 / math.pi) * (ys + 0.044715 * (ys * ys * ys))))
    gl = zz * _sigmoid(_dot(zz.astype(BF16), wglu_ref[...]) + bglu_ref[...])
    gl = gl * lax.rsqrt(jnp.mean(gl * gl, axis=-1, keepdims=True) + NORM_EPS) * gain_ref[...]
    mix = _dot(yr_ref[...].astype(BF16), wo_ref[:D_RWKV, :]) + _dot(gl.astype(BF16), wo_ref[D_RWKV:, :])
    x1 = x_ref[...] + gm_ref[...] * mix
    h = x1 * lax.rsqrt(jnp.mean(x1 * x1, axis=-1, keepdims=True) + NORM_EPS)
    h = (h * (1.0 + scf_ref[...]) + shf_ref[...]).astype(BF16)
    for j in range(D_FF // ff_tile):
        cs = slice(j * ff_tile, (j + 1) * ff_tile)
        gate = _dot(h, wg_ref[:, cs])
        up = _dot(h, wu_ref[:, cs])
        act_scr[:, cs] = (gate * _sigmoid(gate) * up).astype(BF16)
    x2 = x1 + gf_ref[...] * _dot(act_scr[...], wd_ref[...])
    o_ref[...] = x2 * lax.rsqrt(jnp.mean(x2 * x2, axis=-1, keepdims=True) + NORM_EPS) * fg_ref[...]


def _tail(x, y_rwkv, y_s5, g_m, sh_f, sc_f, g_f, w_glu, b_glu, gain, w_out, w_gate, w_up, w_down,
          final_gain, tm, ff_tile):
    bsz, seq, d = x.shape
    once = dict(pipeline_mode=pl.Buffered(1))
    const = lambda shape: pl.BlockSpec(shape, lambda b, t: (0,) * len(shape), **once)
    tok = lambda n: pl.BlockSpec((None, tm, n), lambda b, t: (b, t, 0))
    per_b = pl.BlockSpec((None, 1, d), lambda b, t: (b, 0, 0))
    b3 = lambda a: a.reshape(bsz, 1, d)
    return pl.pallas_call(
        functools.partial(_tail_kernel, ff_tile=ff_tile),
        grid=(bsz, seq // tm),
        in_specs=[tok(d), tok(D_RWKV), pl.BlockSpec((None, tm // S5_L, S5_L * D_S5), lambda b, t: (b, t, 0)),
                  per_b, per_b, per_b, per_b,
                  const((D_S5, D_S5)), const((1, D_S5)), const((1, D_S5)),
                  const((D_RWKV + D_S5, d)), const((d, D_FF)), const((d, D_FF)), const((D_FF, d)),
                  const((1, d))],
        out_specs=tok(d),
        out_shape=jax.ShapeDtypeStruct((bsz, seq, d), F32),
        scratch_shapes=[pltpu.VMEM((S5_NQ, tm, LANES), F32), pltpu.VMEM((tm, D_FF), BF16)],
        compiler_params=pltpu.CompilerParams(dimension_semantics=("parallel", "parallel"),
                                             vmem_limit_bytes=VMEM_LIMIT),
    )(x, y_rwkv, y_s5, b3(g_m), b3(sh_f), b3(sc_f), b3(g_f), w_glu.astype(BF16), b_glu.reshape(1, -1),
      gain.reshape(1, -1), w_out.astype(BF16), w_gate.astype(BF16), w_up.astype(BF16),
      w_down.astype(BF16), final_gain.reshape(1, d))


def kernel(x, c, w_ada, b_ada, w_in, mu_shift, rw_w0, rw_w2, rw_a0, rw_a2, rw_g2, rw_k_k, rw_k_a, rw_r_k,
           rw_lnx_w, rw_lnx_b, s5_a_re, s5_a_im, s5_log_dt, s5_b_re, s5_b_im, s5_c_re, s5_c_im, s5_d,
           s5_w_glu, s5_b_glu, s5_gain, w_out, ffn_w_gate, ffn_w_up, ffn_w_down, final_gain):
    assert w_ada.shape[0] == 1, "single-layer trunk"
    seq = x.shape[1]
    ada = _ada(c, w_ada[0], b_ada[0])
    sh_m, sc_m, g_m, sh_f, sc_f, g_f = jnp.split(ada, 6, axis=-1)
    r, k, v, ld, a, b, g, u = _inproj(x, sh_m, sc_m, w_in[0], mu_shift[0], rw_w0[0], rw_w2[0], rw_a0[0],
                                      rw_a2[0], rw_g2[0], rw_k_k[0], rw_k_a[0], tm=min(512, seq))
    y_rwkv = _rwkv(r, k, v, ld, a, b, g, rw_r_k[0], rw_lnx_w[0], rw_lnx_b[0], tb=min(2048, seq))
    y_s5 = _s5_core(u, s5_a_re[0], s5_a_im[0], s5_log_dt[0], s5_b_re[0], s5_b_im[0], s5_c_re[0],
                    s5_c_im[0], s5_d[0], ct=min(128, seq // S5_L))
    return _tail(x, y_rwkv, y_s5, g_m, sh_f, sc_f, g_f, s5_w_glu[0], s5_b_glu[0], s5_gain[0], w_out[0],
                 ffn_w_gate[0], ffn_w_up[0], ffn_w_down[0], final_gain, tm=min(512, seq), ff_tile=256)
```

```python
import functools
import math

import jax
import jax.numpy as jnp
from jax import lax
from jax.experimental import pallas as pl
from jax.experimental.pallas import tpu as pltpu

F32 = jnp.float32
BF16 = jnp.bfloat16

D_MODEL = 1024
D_RWKV = 512
D_S5 = 512
HEAD = 64
LORA_W = 64
LORA_A = 64
LORA_G = 128
S5_CH = 16
S5_GROUPS = 32
S5_STATE = 64
D_FF = 2816
D_SHIFT = 3 * D_RWKV + LORA_W + LORA_A + LORA_G
D_IN = D_SHIFT + D_S5
NORM_EPS = 1e-6
LNX_EPS = 64e-5

CHUNK = 64
PAIR = 2 * HEAD
INPROJ_SUB = 128
UNIT_CHUNKS = 4
S5_L = 8
LANES = 128
S5_QG = LANES // S5_CH
S5_NQ = D_S5 // LANES
S5_W = S5_L * LANES
S5_SW = S5_QG * 2 * S5_STATE
S5_SROWS = S5_NQ * S5_SW // LANES
VMEM_LIMIT = 56 * 1024 * 1024


def _split_bf16(x, n):
    parts = []
    rem = x
    for i in range(n):
        p = rem.astype(BF16)
        parts.append(p)
        if i + 1 < n:
            rem = rem - p.astype(F32)
    return parts


def _dot(a, b):
    return jnp.dot(a, b, preferred_element_type=F32)


def _dot_nt(a, b):
    return lax.dot_general(a, b, (((1,), (1,)), ((), ())), preferred_element_type=F32)


def _dot_split_lhs(x, rhs_bf16, n):
    acc = None
    for p in _split_bf16(x, n):
        d = _dot(p, rhs_bf16)
        acc = d if acc is None else acc + d
    return acc


def _sigmoid(x):
    return 1.0 / (1.0 + jnp.exp(-x))


def _ada_kernel(c_ref, w_ref, b_ref, o_ref):
    c = c_ref[...]
    act = c * _sigmoid(c)
    o_ref[...] = jnp.dot(act, w_ref[...], preferred_element_type=F32,
                         precision=lax.Precision.HIGHEST) + b_ref[...]


def _ada(c, w_ada, b_ada):
    bsz, d = c.shape
    rows = 8
    c_pad = jnp.zeros((rows, d), F32).at[:bsz].set(c)
    n_out = w_ada.shape[1]
    out = pl.pallas_call(
        _ada_kernel,
        grid=(n_out // d,),
        in_specs=[
            pl.BlockSpec((rows, d), lambda j: (0, 0)),
            pl.BlockSpec((d, d), lambda j: (0, j)),
            pl.BlockSpec((1, d), lambda j: (0, j)),
        ],
        out_specs=pl.BlockSpec((rows, d), lambda j: (0, j)),
        out_shape=jax.ShapeDtypeStruct((rows, n_out), F32),
        compiler_params=pltpu.CompilerParams(dimension_semantics=("arbitrary",)),
    )(c_pad, w_ada, b_ada.reshape(1, n_out))
    return out[:bsz]


def _inproj_kernel(x_ref, sh_ref, sc_ref, win_ref, mu_ref, w0_ref, w2_ref, a0_ref, a2_ref, g2_ref,
                   kk_ref, ka_ref, bd_ref,
                   r_out, k_out, v_out, ld_out, a_out, b_out, g_out, u_out, carry_ref, u_scr):
    @pl.when(pl.program_id(1) == 0)
    def _():
        carry_ref[...] = jnp.zeros_like(carry_ref)

    sub = INPROJ_SUB
    prev_last = carry_ref[...]
    for s in range(x_ref.shape[0] // sub):
        rows = slice(s * sub, (s + 1) * sub)
        crow = slice(s * (sub // S5_L), (s + 1) * (sub // S5_L))
        prev_last = _inproj_rows(x_ref[rows, :], prev_last, rows, crow, sh_ref, sc_ref, win_ref, mu_ref,
                                 w0_ref, w2_ref, a0_ref, a2_ref, g2_ref, kk_ref, ka_ref, bd_ref,
                                 r_out, k_out, v_out, ld_out, a_out, b_out, g_out, u_out, u_scr)
    carry_ref[...] = prev_last


def _inproj_rows(x, prev_last, rows, crow, sh_ref, sc_ref, win_ref, mu_ref, w0_ref, w2_ref, a0_ref, a2_ref,
                 g2_ref, kk_ref, ka_ref, bd_ref, r_out, k_out, v_out, ld_out, a_out, b_out, g_out, u_out, u_scr):
    tm = x.shape[0]
    ms = jnp.mean(x * x, axis=-1, keepdims=True)
    h = x * lax.rsqrt(ms + NORM_EPS)
    h = h * (1.0 + sc_ref[...]) + sh_ref[...]
    proj = _dot(h.astype(BF16), win_ref[...])
    z = proj[:, :D_SHIFT]
    for k in range(S5_NQ):
        u_scr[k, rows, :] = proj[:, D_SHIFT + k * LANES:D_SHIFT + (k + 1) * LANES]
    for tl in range(S5_L):
        for k in range(S5_NQ):
            lo = tl * D_S5 + k * LANES
            picked = u_scr[k, pl.ds(rows.start + tl, tm // S5_L, stride=S5_L), :]
            u_out[crow, lo:lo + LANES] = picked.astype(u_out.dtype)

    z_roll = pltpu.roll(z, 1, axis=0)
    row = lax.broadcasted_iota(jnp.int32, z.shape, 0)
    z_prev = jnp.where(row == 0, prev_last, z_roll)
    zz = z + mu_ref[...] * (z_prev - z)

    r = zz[:, 0:D_RWKV]
    k = zz[:, D_RWKV:2 * D_RWKV]
    v = zz[:, 2 * D_RWKV:3 * D_RWKV]
    o = 3 * D_RWKV
    w_lo = zz[:, o:o + LORA_W]
    a_lo = zz[:, o + LORA_W:o + LORA_W + LORA_A]
    g_lo = zz[:, o + LORA_W + LORA_A:o + LORA_W + LORA_A + LORA_G]

    wl = w0_ref[...] + _dot(jnp.tanh(w_lo).astype(BF16), w2_ref[...])
    ld_out[rows, :] = -math.exp(-0.5) * _sigmoid(wl)
    a = _sigmoid(a0_ref[...] + _dot(a_lo.astype(BF16), a2_ref[...]))
    g_out[rows, :] = _dot(_sigmoid(g_lo).astype(BF16), g2_ref[...]).astype(g_out.dtype)

    kk = k * kk_ref[...]
    n2 = _dot((kk * kk).astype(BF16), bd_ref[...])
    kk = kk / jnp.maximum(jnp.sqrt(n2), 1e-12)
    r_out[rows, :] = r.astype(r_out.dtype)
    k_out[rows, :] = (k * (1.0 + (a - 1.0) * ka_ref[...])).astype(k_out.dtype)
    v_out[rows, :] = v.astype(v_out.dtype)
    a_out[rows, :] = (-kk).astype(a_out.dtype)
    b_out[rows, :] = (kk * a).astype(b_out.dtype)
    return z[tm - 1:tm, :]


def _head_block_diag(n, dtype, value=1.0):
    i = jnp.arange(n) // HEAD
    return jnp.where(i[:, None] == i[None, :], value, 0.0).astype(dtype)


def _inproj(x, sh_m, sc_m, w_in, mu, w0, w2, a0, a2, g2, k_k, k_a, tm):
    bsz, seq, d = x.shape
    row = lambda a: a.reshape(1, -1)
    const = lambda shape: pl.BlockSpec(shape, lambda b, t: (0,) * len(shape), pipeline_mode=pl.Buffered(1))
    tok = lambda n: pl.BlockSpec((None, tm, n), lambda b, t: (b, t, 0))
    per_b = pl.BlockSpec((None, 1, d), lambda b, t: (b, 0, 0))
    out_sds = lambda dt: jax.ShapeDtypeStruct((bsz, seq, D_RWKV), dt)
    out_dtypes = [BF16, BF16, BF16, F32, BF16, BF16, BF16]
    return pl.pallas_call(
        _inproj_kernel,
        grid=(bsz, seq // tm),
        in_specs=[tok(d), per_b, per_b, const((d, D_IN)), const((1, D_SHIFT)),
                  const((1, D_RWKV)), const((LORA_W, D_RWKV)), const((1, D_RWKV)),
                  const((LORA_A, D_RWKV)), const((LORA_G, D_RWKV)), const((1, D_RWKV)),
                  const((1, D_RWKV)), const((D_RWKV, D_RWKV))],
        out_specs=[tok(D_RWKV)] * 7 + [pl.BlockSpec((None, tm // S5_L, S5_L * D_S5), lambda b, t: (b, t, 0))],
        out_shape=[out_sds(dt) for dt in out_dtypes]
        + [jax.ShapeDtypeStruct((bsz, seq // S5_L, S5_L * D_S5), BF16)],
        scratch_shapes=[pltpu.VMEM((1, D_SHIFT), F32), pltpu.VMEM((S5_NQ, tm, LANES), F32)],
        compiler_params=pltpu.CompilerParams(dimension_semantics=("parallel", "arbitrary"),
                                             vmem_limit_bytes=VMEM_LIMIT),
    )(x, sh_m.reshape(bsz, 1, d), sc_m.reshape(bsz, 1, d), w_in.astype(BF16), row(mu),
      row(w0), w2.astype(BF16), row(a0), a2.astype(BF16), g2.astype(BF16), row(k_k), row(k_a),
      _head_block_diag(D_RWKV, BF16))


def _per_head(yb, lo, hi):
    return jnp.concatenate([yb * lo, yb * hi], axis=0)


def _rwkv_prep(slabs, rk_rows, cst):
    row, strict, incl, bd, eye, m0, m1, m0w, m1w, eye_c, merge_masks = cst
    c = CHUNK
    r, k, v, ld, a, b = ([s[i] for s in slabs] for i in range(6))
    cum = ld
    sft = 1
    while sft < c:
        if sft < 8:
            cum = [x + jnp.where(row >= sft, pltpu.roll(x, sft, axis=0), 0.0) for x in cum]
        else:
            pad = jnp.zeros((sft, PAIR), F32)
            cum = [x + jnp.concatenate([pad, x[:c - sft]], axis=0) for x in cum]
        sft *= 2
    cl = [x[c - 1:c, :] for x in cum]
    w_to = [jnp.exp(x) for x in cum]
    w_inv = [jnp.exp(-x) for x in cum]
    w_prev = [jnp.exp(x - y) for x, y in zip(cum, ld)]
    w_rem = [jnp.exp(y - x) for x, y in zip(cum, cl)]
    w_all = [jnp.exp(y) for y in cl]
    rt = [x * w for x, w in zip(r, w_to)]
    kt = [x * w for x, w in zip(k, w_inv)]
    at = [x * w for x, w in zip(a, w_prev)]
    bt = [x * w for x, w in zip(b, w_inv)]
    bh = [x * w for x, w in zip(b, w_rem)]
    kh = [x * w for x, w in zip(k, w_rem)]
    lhs = [jnp.concatenate([x, y], axis=0).astype(BF16) for x, y in zip(at, rt)]
    rhs = [jnp.concatenate([_per_head(x.astype(BF16), m0, m1), _per_head(y.astype(BF16), m0, m1)], axis=0)
           for x, y in zip(bt, kt)]
    vb = [x.astype(BF16) for x in v]
    bk_t = [jnp.concatenate([x, y], axis=0).T.astype(BF16) for x, y in zip(bh, kh)]
    rkb = [(x * y * z).astype(BF16) for x, y, z in zip(r, k, rk_rows)]
    return list(zip(lhs, rhs, vb, at, rt, bk_t, w_all, rkb))


def _rwkv_chunk_maps(ops, cst):
    row, strict, incl, bd, eye, m0, m1, m0w, m1w, eye_c, merge_masks = cst
    c = CHUNK
    lhs, rhs, vb, at, rt, bk_t, w_all, _ = ([o[i] for o in ops] for i in range(8))
    per_head = _per_head
    a_all = [_dot_nt(x, y) for x, y in zip(lhs, rhs)]
    n_ab = [jnp.where(strict, x[:c, :PAIR], 0.0) for x in a_all]
    a_ak = [jnp.where(strict, x[:c, PAIR:], 0.0) for x in a_all]
    a_rb = [jnp.where(incl, x[c:, :PAIR], 0.0) for x in a_all]
    a_rk = [jnp.where(incl, x[c:, PAIR:], 0.0) for x in a_all]

    vs = [per_head(x, m0, m1) for x in vb]
    x0 = [jnp.concatenate([y, _dot(z.astype(BF16), w)], axis=1) for y, z, w in zip(at, a_ak, vs)]

    nb = [x.astype(BF16) for x in n_ab]
    t_inv = [eye_c + x * merge_masks[0] for x in n_ab]
    anchors = [None] * len(ops)
    for lvl, msk in enumerate(merge_masks[1:]):
        sz = 2 << lvl
        tb = [t.astype(BF16) for t in t_inv]
        if sz % 16:
            tmp = [_dot(t, per_head(n * msk, m0, m1)) for t, n in zip(tb, nb)]
            t_inv = [t + _dot(m.astype(BF16), per_head(h, m0, m1)) for t, m, h in zip(t_inv, tmp, tb)]
        else:
            low = [lo for lo in range(0, c, sz) if (lo // sz) % 2]
            pick = lambda y: jnp.concatenate([y[lo:lo + sz] for lo in low], axis=0)
            tmp = [_dot(pick(t), per_head(n * msk, m0, m1)) for t, n in zip(tb, nb)]
            upd = [_dot(m.astype(BF16), per_head(h, m0, m1)) for m, h in zip(tmp, tb)]

            def put_back(t, u):
                blocks = [t[lo:lo + sz] for lo in range(0, c, sz)]
                for j, lo in enumerate(low):
                    blocks[lo // sz] = blocks[lo // sz] + u[j * sz:(j + 1) * sz]
                return jnp.concatenate(blocks, axis=0)

            t_inv = [put_back(t, u) for t, u in zip(t_inv, upd)]
        for n in range(len(ops)):
            if lvl == (n * (len(merge_masks) - 1)) // len(ops):
                anchors[n] = tmp[n][:8]
    x1 = [_dot(t.astype(BF16), per_head(y.astype(BF16), m0w, m1w)) for t, y in zip(t_inv, x0)]
    x1b = [x.astype(BF16) for x in x1]
    rx = [_dot(z.astype(BF16), per_head(y, m0w, m1w)) for z, y in zip(a_rb, x1b)]
    r_hat = [x + y[:, :PAIR] for x, y in zip(rt, rx)]
    y_hat = [y[:, PAIR:] + _dot(z.astype(BF16), w) for y, z, w in zip(rx, a_rk, vs)]
    zero = jnp.zeros((c, PAIR), BF16)
    gm = [_dot(x, jnp.concatenate([y, jnp.concatenate([zero, z], axis=1)], axis=0))
          for x, y, z in zip(bk_t, x1b, vb)]
    m_mat = [eye * w + x[:, :PAIR] * bd for w, x in zip(w_all, gm)]
    s_hat = [x[:, PAIR:] * bd for x in gm]
    return list(zip(r_hat, y_hat, m_mat, s_hat)), anchors


def _rwkv_kernel(r_ref, k_ref, v_ref, ld_ref, a_ref, b_ref, g_ref, rk_ref, lw_ref, lb_ref,
                 bd_ref, avg2_ref, stat2_ref, y_ref, s_ref, *prep_refs):
    t = pl.program_id(1)

    @pl.when(t == 0)
    def _():
        s_ref[...] = jnp.zeros_like(s_ref)

    c = CHUNK
    lane = lax.broadcasted_iota(jnp.int32, (1, PAIR), 1)
    m0 = (lane < HEAD).astype(F32).astype(BF16)
    m1 = (lane >= HEAD).astype(F32).astype(BF16)
    lane_w = lax.broadcasted_iota(jnp.int32, (1, 2 * PAIR), 1) % PAIR
    m0w = (lane_w < HEAD).astype(F32).astype(BF16)
    m1w = (lane_w >= HEAD).astype(F32).astype(BF16)
    row = lax.broadcasted_iota(jnp.int32, (c, PAIR), 0)
    col = lax.broadcasted_iota(jnp.int32, (c, PAIR), 1) % c
    strict = col < row
    incl = col <= row
    bd = bd_ref[...]
    ri = lax.broadcasted_iota(jnp.int32, (PAIR, PAIR), 0)
    ci = lax.broadcasted_iota(jnp.int32, (PAIR, PAIR), 1)
    eye = (ri == ci).astype(F32)
    eye_c = (col == row).astype(F32)
    merge_masks = []
    sz = 1
    while sz < c:
        msk = (row // (2 * sz) == col // (2 * sz)) & (row % (2 * sz) >= sz) & (col % (2 * sz) < sz)
        merge_masks.append(msk.astype(F32) if sz == 1 else msk.astype(F32).astype(BF16))
        sz *= 2
    cst = (row, strict, incl, bd, eye, m0, m1, m0w, m1w, eye_c, merge_masks)

    n_pairs = D_RWKV // PAIR
    span = UNIT_CHUNKS * c
    units = [(ch, p) for ch in range(UNIT_CHUNKS) for p in range(n_pairs)]
    lanes = [slice(p * PAIR, (p + 1) * PAIR) for _, p in units]
    n_blocks = r_ref.shape[0] // span
    slot_a, slot_b = prep_refs[:len(prep_refs) // 2], prep_refs[len(prep_refs) // 2:]

    def unit_rows(blk):
        base = pl.multiple_of(blk * span, span)
        return [pl.ds(base + ch * c, c) for ch, _ in units]

    def exact_zero(x):
        bits = pltpu.bitcast(x, jnp.uint32)
        half = jnp.uint32(16)
        return pltpu.bitcast(lax.shift_right_logical(lax.shift_right_logical(bits, half), half), F32)

    def prep_block(blk, slot, after=None):
        slabs = [tuple(ref[rs, ls].astype(F32) for ref in (r_ref, k_ref, v_ref, ld_ref, a_ref, b_ref))
                 for rs, ls in zip(unit_rows(blk), lanes)]
        if after is not None:
            tied = [jnp.concatenate([s[3][:8] + exact_zero(x), s[3][8:]], axis=0) for s, x in zip(slabs, after)]
            slabs = [s[:3] + (ld,) + s[4:] for s, ld in zip(slabs, tied)]
        ops = _rwkv_prep(slabs, [rk_ref[:, ls] for ls in lanes], cst)
        for n, op in enumerate(ops):
            for ref, val in zip(slot, op):
                ref[n] = val

    def main_block(blk, slot):
        ops = [tuple(ref[n] for ref in slot) for n in range(len(units))]
        maps, anchors = _rwkv_chunk_maps(ops, cst)

        state = [s_ref[p] for p in range(n_pairs)]
        ys = []
        for (ch, p), (r_hat, y_hat, m_mat, s_hat) in zip(units, maps):
            sb = state[p].astype(BF16)
            ym = _dot(jnp.concatenate([r_hat, m_mat], axis=0).astype(BF16), sb)
            ys.append(ym[:c] + y_hat)
            state[p] = ym[c:] + s_hat
        for p in range(n_pairs):
            s_ref[p] = state[p]

        y_all = jnp.concatenate(ys, axis=0)
        dlt = y_all - _dot(jnp.concatenate(_split_bf16(y_all, 2), axis=1), avg2_ref[...])
        rkb = jnp.concatenate([op[7] for op in ops], axis=0)
        stats = _dot(jnp.concatenate([(dlt * dlt).astype(BF16), rkb], axis=1), stat2_ref[...])
        yn = dlt * lax.rsqrt(stats[:, :PAIR] + LNX_EPS)
        bonus = stats[:, PAIR:]
        for n, (op, rs, ls) in enumerate(zip(ops, unit_rows(blk), lanes)):
            un = slice(n * c, (n + 1) * c)
            y = yn[un] * lw_ref[:, ls] + lb_ref[:, ls] + bonus[un] * op[2].astype(F32)
            y_ref[rs, ls] = (y * g_ref[rs, ls].astype(F32)).astype(y_ref.dtype)
        return anchors

    prep_block(0, slot_a)

    def body(j, carry):
        done = main_block(2 * j, slot_a)
        prep_block(2 * j + 1, slot_b, after=done)
        done = main_block(2 * j + 1, slot_b)
        prep_block(jnp.minimum(2 * j + 2, n_blocks - 1), slot_a, after=done)
        return carry

    lax.fori_loop(0, n_blocks // 2, body, 0)


def _rwkv(r, k, v, ld, a, b, g, r_k, lnx_w, lnx_b, tb):
    bsz, seq, d = r.shape
    tok = pl.BlockSpec((None, tb, d), lambda i, t: (i, t, 0))
    const = lambda shape: pl.BlockSpec(shape, lambda i, t: (0,) * len(shape))
    assert (tb // (UNIT_CHUNKS * CHUNK)) % 2 == 0, "the block pipeline is written for an even block count"
    c, nu = CHUNK, UNIT_CHUNKS * (d // PAIR)
    slot = [pltpu.VMEM((nu, 2 * c, PAIR), BF16), pltpu.VMEM((nu, 4 * c, PAIR), BF16),
            pltpu.VMEM((nu, c, PAIR), BF16), pltpu.VMEM((nu, c, PAIR), F32), pltpu.VMEM((nu, c, PAIR), F32),
            pltpu.VMEM((nu, PAIR, 2 * c), BF16), pltpu.VMEM((nu, 1, PAIR), F32), pltpu.VMEM((nu, c, PAIR), BF16)]
    avg = _head_block_diag(PAIR, BF16, 1.0 / HEAD)
    ones = _head_block_diag(PAIR, BF16)
    zero = jnp.zeros((PAIR, PAIR), BF16)
    avg2 = jnp.concatenate([avg, avg], axis=0)
    stat2 = jnp.concatenate([jnp.concatenate([avg, zero], axis=1), jnp.concatenate([zero, ones], axis=1)], axis=0)
    return pl.pallas_call(
        _rwkv_kernel,
        grid=(bsz, seq // tb),
        in_specs=[tok] * 7 + [const((1, d))] * 3
        + [const((PAIR, PAIR)), const((2 * PAIR, PAIR)), const((2 * PAIR, 2 * PAIR))],
        out_specs=tok,
        out_shape=jax.ShapeDtypeStruct((bsz, seq, d), BF16),
        scratch_shapes=[pltpu.VMEM((d // PAIR, PAIR, PAIR), F32)] + slot + slot,
        compiler_params=pltpu.CompilerParams(dimension_semantics=("parallel", "arbitrary"),
                                             vmem_limit_bytes=VMEM_LIMIT),
    )(r, k, v, ld, a, b, g, r_k.reshape(1, d), lnx_w.reshape(1, d), lnx_b.reshape(1, d),
      _head_block_diag(PAIR, F32), avg2, stat2)


def _s5_prep_kernel(are_ref, aim_ref, ldt_ref, btre_ref, btim_ref, cre_ref, cim_ref, dsk_ref,
                    qt_out, p_out, t_out, lam_out):
    for gi in range(are_ref.shape[0]):
        _s5_prep_group(are_ref[gi], aim_ref[gi], ldt_ref[gi], btre_ref[gi], btim_ref[gi], cre_ref[gi],
                       cim_ref[gi], dsk_ref[gi], qt_out.at[gi], p_out.at[gi], t_out.at[gi], lam_out.at[gi])


def _s5_prep_group(a_re, a_im, log_dt, bt_re, bt_im, c_re, c_im, d_tile, qt_out, p_out, t_out, lam_out):
    dt = jnp.exp(log_dt)
    nd = 24
    dpow = lax.broadcasted_iota(jnp.int32, (nd, S5_STATE), 0).astype(F32)
    mag = jnp.exp(dpow * (dt * a_re))
    ang = dpow * (dt * a_im)
    e_re = mag * jnp.cos(ang)
    e_im = mag * jnp.sin(ang)
    lam_re = e_re[1:2, :]
    lam_im = e_im[1:2, :]
    den = a_re * a_re + a_im * a_im
    pp = lam_re - 1.0
    qq = lam_im
    coef_re = (pp * a_re + qq * a_im) / den
    coef_im = (qq * a_re - pp * a_im) / den
    bb_re = coef_re * bt_re - coef_im * bt_im
    bb_im = coef_re * bt_im + coef_im * bt_re
    c_lam = []
    for d in range(S5_L + 1):
        er = e_re[d:d + 1, :]
        ei = e_im[d:d + 1, :]
        c_lam.append(jnp.concatenate([c_re * er - c_im * ei, -(c_re * ei + c_im * er)], axis=1))
    qt_out[...] = jnp.concatenate(c_lam[1:], axis=0)
    for j in range(S5_L):
        d = S5_L - 1 - j
        er = e_re[d:d + 1, :]
        ei = e_im[d:d + 1, :]
        p_out[j * S5_CH:(j + 1) * S5_CH, :] = jnp.concatenate(
            [er * bb_re - ei * bb_im, er * bb_im + ei * bb_re], axis=1)
    y0 = jnp.concatenate([bb_re, bb_im], axis=1)
    t_rows = []
    for j in range(S5_L):
        lagged = jnp.concatenate([jnp.zeros_like(c_lam[0])] * j + c_lam[:S5_L - j], axis=0)
        t_rows.append(lax.dot_general(y0, lagged, (((1,), (1,)), ((), ())), preferred_element_type=F32,
                                      precision=lax.Precision.HIGHEST))
    gw = S5_L * S5_CH
    on_diag = lax.broadcasted_iota(jnp.int32, (gw, gw), 0) == lax.broadcasted_iota(jnp.int32, (gw, gw), 1)
    t_out[...] = jnp.concatenate(t_rows, axis=0) + jnp.where(on_diag, d_tile, 0.0)
    lam_out[...] = jnp.concatenate([e_re[S5_L:S5_L + 1, :], e_im[S5_L:S5_L + 1, :]], axis=1)


def _s5_prep(a_re, a_im, log_dt, b_re, b_im, c_re, c_im, d_skip):
    g, p = a_re.shape
    per_g = lambda shape: pl.BlockSpec((S5_QG,) + shape, lambda i: (i,) + (0,) * len(shape))
    gw = S5_L * S5_CH
    return pl.pallas_call(
        _s5_prep_kernel,
        grid=(g // S5_QG,),
        in_specs=[per_g((1, p)), per_g((1, p)), per_g((1, 1)), per_g((S5_CH, p)), per_g((S5_CH, p)),
                  per_g((S5_CH, p)), per_g((S5_CH, p)), per_g((1, gw))],
        out_specs=[per_g((gw, 2 * p)), per_g((gw, 2 * p)), per_g((gw, gw)), per_g((1, 2 * p))],
        out_shape=[jax.ShapeDtypeStruct((g, gw, 2 * p), F32),
                   jax.ShapeDtypeStruct((g, gw, 2 * p), F32),
                   jax.ShapeDtypeStruct((g, gw, gw), F32),
                   jax.ShapeDtypeStruct((g, 1, 2 * p), F32)],
        compiler_params=pltpu.CompilerParams(dimension_semantics=("arbitrary",)),
    )(a_re.reshape(g, 1, p), a_im.reshape(g, 1, p), log_dt.reshape(g, 1, 1),
      jnp.swapaxes(b_re, 1, 2), jnp.swapaxes(b_im, 1, 2), c_re, c_im,
      jnp.tile(d_skip, (1, S5_L)).reshape(g, 1, gw))


def _s5_weights_kernel(p_ref, t_ref, qt_ref, perm_ref, bp_out, bt_out, bq_out, rows_ok_ref):
    gw = p_ref.shape[1]
    for h in range(p_ref.shape[0]):
        cols = slice(h * gw, (h + 1) * gw)
        perm_h = perm_ref[:, cols]
        bp_out[:, cols] = _dot(perm_h, p_ref[h].astype(BF16)).astype(BF16)
        rows_ok_ref[:, cols] = _dot(perm_h, t_ref[h].astype(BF16)).astype(BF16)
        bq_out[cols, :] = _dot_nt(qt_ref[h].T.astype(BF16), perm_h).astype(BF16)
    bt_out[...] = _dot_nt(rows_ok_ref[...], perm_ref[...]).astype(BF16)


def _s5_weights(p_all, t_all, qt_all):
    g, gw, sw = p_all.shape
    dst = jnp.arange(S5_W)
    j, h, a = dst // LANES, (dst % LANES) // S5_CH, dst % S5_CH
    src = h * gw + j * S5_CH + a
    perm = (src[:, None] == jnp.arange(S5_W)[None, :]).astype(BF16)
    grp = lambda n: pl.BlockSpec((S5_QG, gw, n), lambda q: (q, 0, 0))
    big = lambda r, c: pl.BlockSpec((None, r, c), lambda q: (q, 0, 0))
    return pl.pallas_call(
        _s5_weights_kernel,
        grid=(S5_NQ,),
        in_specs=[grp(sw), grp(gw), grp(sw), pl.BlockSpec((S5_W, S5_W), lambda q: (0, 0))],
        out_specs=[big(S5_W, S5_SW), big(S5_W, S5_W), big(S5_SW, S5_W)],
        out_shape=[jax.ShapeDtypeStruct((S5_NQ, S5_W, S5_SW), BF16),
                   jax.ShapeDtypeStruct((S5_NQ, S5_W, S5_W), BF16),
                   jax.ShapeDtypeStruct((S5_NQ, S5_SW, S5_W), BF16)],
        scratch_shapes=[pltpu.VMEM((S5_W, S5_W), BF16)],
        compiler_params=pltpu.CompilerParams(dimension_semantics=("arbitrary",),
                                             vmem_limit_bytes=VMEM_LIMIT),
    )(p_all, t_all, qt_all, perm)


def _s5_kernel(u_ref, bp_ref, bt_ref, bq_ref, lr_ref, li_ref, y_ref, z_ref, s_ref):
    @pl.when(pl.program_id(0) == 0)
    def _():
        s_ref[...] = jnp.zeros_like(s_ref)

    nb, ct, _ = u_ref.shape
    rows = nb * ct
    kb = S5_SW // LANES

    def cols(q):
        return [slice(tl * D_S5 + q * LANES, tl * D_S5 + (q + 1) * LANES) for tl in range(S5_L)]

    def x_of(q):
        return jnp.concatenate([u_ref[:, :, cs].reshape(rows, LANES) for cs in cols(q)], axis=1)

    for q in range(S5_NQ):
        z = _dot(x_of(q), bp_ref[q])
        z_ref[:, q * kb:(q + 1) * kb, :] = z.reshape(rows, kb, LANES)

    lr = lr_ref[...]
    li = li_ref[...]

    def step(c, carry):
        s, s_sw = carry
        z = jnp.concatenate([z_ref[b * ct + c] for b in range(nb)], axis=0)
        for b in range(nb):
            z_ref[b * ct + c] = s[b * S5_SROWS:(b + 1) * S5_SROWS]
        z_sw = pltpu.roll(z, S5_STATE, axis=1)
        return lr * s + li * s_sw + z, lr * s_sw - li * s + z_sw

    s0 = s_ref[...]
    s_end, _ = lax.fori_loop(0, ct, step, (s0, pltpu.roll(s0, S5_STATE, axis=1)), unroll=4)
    s_ref[...] = s_end

    for q in range(S5_NQ):
        start = z_ref[:, q * kb:(q + 1) * kb, :].reshape(rows, kb * LANES).astype(BF16)
        x = x_of(q)
        pair = 2 * LANES
        local = jnp.concatenate([_dot(x[:, :(i + 1) * pair], bt_ref[q, :(i + 1) * pair, i * pair:(i + 1) * pair])
                                 for i in range(S5_W // pair)], axis=1)
        y = (local + _dot(start, bq_ref[q])).astype(y_ref.dtype)
        for tl, cs in enumerate(cols(q)):
            y_ref[:, :, cs] = y[:, tl * LANES:(tl + 1) * LANES].reshape(nb, ct, LANES)


def _s5_core(u2, a_re, a_im, log_dt, b_re, b_im, c_re, c_im, d_skip, ct):
    bsz, nck, _ = u2.shape
    p, ll, nq = S5_STATE, S5_L, S5_NQ
    qt_all, p_all, t_all, lam = _s5_prep(a_re, a_im, log_dt, b_re, b_im, c_re, c_im, d_skip)
    big_p, big_t, big_q = _s5_weights(p_all, t_all, qt_all)
    lam_re = lam[:, 0, :p]
    lam_im = lam[:, 0, p:]
    lr = jnp.tile(jnp.concatenate([lam_re, lam_re], axis=1), (bsz, 1))
    li = jnp.tile(jnp.concatenate([-lam_im, lam_im], axis=1), (bsz, 1))

    once = dict(pipeline_mode=pl.Buffered(1))
    const = lambda shape: pl.BlockSpec(shape, lambda i: (0,) * len(shape), **once)
    tok = pl.BlockSpec((bsz, ct, ll * D_S5), lambda i: (0, i, 0))
    y2 = pl.pallas_call(
        _s5_kernel,
        grid=(nck // ct,),
        in_specs=[tok, const((nq, S5_W, S5_SW)), const((nq, S5_W, S5_W)), const((nq, S5_SW, S5_W)),
                  const((bsz * S5_SROWS, LANES)), const((bsz * S5_SROWS, LANES))],
        out_specs=tok,
        out_shape=jax.ShapeDtypeStruct((bsz, nck, ll * D_S5), BF16),
        scratch_shapes=[pltpu.VMEM((bsz * ct, S5_SROWS, LANES), F32),
                        pltpu.VMEM((bsz * S5_SROWS, LANES), F32)],
        compiler_params=pltpu.CompilerParams(dimension_semantics=("arbitrary",),
                                             vmem_limit_bytes=VMEM_LIMIT),
    )(u2, big_p, big_t, big_q, lr, li)
    return y2


def _tail_kernel(x_ref, yr_ref, ys_ref, gm_ref, shf_ref, scf_ref, gf_ref, wglu_ref, bglu_ref, gain_ref,
                 wo_ref, wg_ref, wu_ref, wd_ref, fg_ref, o_ref, ys_scr, act_scr, *, ff_tile):
    tm = x_ref.shape[0]
    for tl in range(S5_L):
        for k in range(S5_NQ):
            lo = tl * D_S5 + k * LANES
            ys_scr[k, pl.ds(tl, tm // S5_L, stride=S5_L), :] = ys_ref[:, lo:lo + LANES].astype(F32)
    ys = jnp.concatenate([ys_scr[k] for k in range(S5_NQ)], axis=1)
    zz = 0.5 * ys * (1.0 + jnp.tanh(math.sqrt(2.0 / math.pi) * (ys + 0.044715 * (ys * ys * ys))))
    gl = zz * _sigmoid(_dot(zz.astype(BF16), wglu_ref[...]) + bglu_ref[...])
    gl = gl * lax.rsqrt(jnp.mean(gl * gl, axis=-1, keepdims=True) + NORM_EPS) * gain_ref[...]
    mix = _dot(yr_ref[...].astype(BF16), wo_ref[:D_RWKV, :]) + _dot(gl.astype(BF16), wo_ref[D_RWKV:, :])
    x1 = x_ref[...] + gm_ref[...] * mix
    h = x1 * lax.rsqrt(jnp.mean(x1 * x1, axis=-1, keepdims=True) + NORM_EPS)
    h = (h * (1.0 + scf_ref[...]) + shf_ref[...]).astype(BF16)
    for j in range(D_FF // ff_tile):
        cs = slice(j * ff_tile, (j + 1) * ff_tile)
        gate = _dot(h, wg_ref[:, cs])
        up = _dot(h, wu_ref[:, cs])
        act_scr[:, cs] = (gate * _sigmoid(gate) * up).astype(BF16)
    x2 = x1 + gf_ref[...] * _dot(act_scr[...], wd_ref[...])
    o_ref[...] = x2 * lax.rsqrt(jnp.mean(x2 * x2, axis=-1, keepdims=True) + NORM_EPS) * fg_ref[...]


def _tail(x, y_rwkv, y_s5, g_m, sh_f, sc_f, g_f, w_glu, b_glu, gain, w_out, w_gate, w_up, w_down,
          final_gain, tm, ff_tile):
    bsz, seq, d = x.shape
    once = dict(pipeline_mode=pl.Buffered(1))
    const = lambda shape: pl.BlockSpec(shape, lambda b, t: (0,) * len(shape), **once)
    tok = lambda n: pl.BlockSpec((None, tm, n), lambda b, t: (b, t, 0))
    per_b = pl.BlockSpec((None, 1, d), lambda b, t: (b, 0, 0))
    b3 = lambda a: a.reshape(bsz, 1, d)
    return pl.pallas_call(
        functools.partial(_tail_kernel, ff_tile=ff_tile),
        grid=(bsz, seq // tm),
        in_specs=[tok(d), tok(D_RWKV), pl.BlockSpec((None, tm // S5_L, S5_L * D_S5), lambda b, t: (b, t, 0)),
                  per_b, per_b, per_b, per_b,
                  const((D_S5, D_S5)), const((1, D_S5)), const((1, D_S5)),
                  const((D_RWKV + D_S5, d)), const((d, D_FF)), const((d, D_FF)), const((D_FF, d)),
                  const((1, d))],
        out_specs=tok(d),
        out_shape=jax.ShapeDtypeStruct((bsz, seq, d), F32),
        scratch_shapes=[pltpu.VMEM((S5_NQ, tm, LANES), F32), pltpu.VMEM((tm, D_FF), BF16)],
        compiler_params=pltpu.CompilerParams(dimension_semantics=("parallel", "parallel"),
                                             vmem_limit_bytes=VMEM_LIMIT),
    )(x, y_rwkv, y_s5, b3(g_m), b3(sh_f), b3(sc_f), b3(g_f), w_glu.astype(BF16), b_glu.reshape(1, -1),
      gain.reshape(1, -1), w_out.astype(BF16), w_gate.astype(BF16), w_up.astype(BF16),
      w_down.astype(BF16), final_gain.reshape(1, d))


def kernel(x, c, w_ada, b_ada, w_in, mu_shift, rw_w0, rw_w2, rw_a0, rw_a2, rw_g2, rw_k_k, rw_k_a, rw_r_k,
           rw_lnx_w, rw_lnx_b, s5_a_re, s5_a_im, s5_log_dt, s5_b_re, s5_b_im, s5_c_re, s5_c_im, s5_d,
           s5_w_glu, s5_b_glu, s5_gain, w_out, ffn_w_gate, ffn_w_up, ffn_w_down, final_gain):
    assert w_ada.shape[0] == 1, "single-layer trunk"
    seq = x.shape[1]
    ada = _ada(c, w_ada[0], b_ada[0])
    sh_m, sc_m, g_m, sh_f, sc_f, g_f = jnp.split(ada, 6, axis=-1)
    r, k, v, ld, a, b, g, u = _inproj(x, sh_m, sc_m, w_in[0], mu_shift[0], rw_w0[0], rw_w2[0], rw_a0[0],
                                      rw_a2[0], rw_g2[0], rw_k_k[0], rw_k_a[0], tm=min(512, seq))
    y_rwkv = _rwkv(r, k, v, ld, a, b, g, rw_r_k[0], rw_lnx_w[0], rw_lnx_b[0], tb=min(2048, seq))
    y_s5 = _s5_core(u, s5_a_re[0], s5_a_im[0], s5_log_dt[0], s5_b_re[0], s5_b_im[0], s5_c_re[0],
                    s5_c_im[0], s5_d[0], ct=min(128, seq // S5_L))
    return _tail(x, y_rwkv, y_s5, g_m, sh_f, sc_f, g_f, s5_w_glu[0], s5_b_glu[0], s5_gain[0], w_out[0],
                 ffn_w_gate[0], ffn_w_up[0], ffn_w_down[0], final_gain, tm=min(512, seq), ff_tile=256)
```

```python
import functools
import math

import jax
import jax.numpy as jnp
from jax import lax
from jax.experimental import pallas as pl
from jax.experimental.pallas import tpu as pltpu

F32 = jnp.float32
BF16 = jnp.bfloat16

D_MODEL = 1024
D_RWKV = 512
D_S5 = 512
HEAD = 64
LORA_W = 64
LORA_A = 64
LORA_G = 128
S5_CH = 16
S5_GROUPS = 32
S5_STATE = 64
D_FF = 2816
D_SHIFT = 3 * D_RWKV + LORA_W + LORA_A + LORA_G
D_IN = D_SHIFT + D_S5
NORM_EPS = 1e-6
LNX_EPS = 64e-5

CHUNK = 64
PAIR = 2 * HEAD
INPROJ_SUB = 128
RWKV_BLOCKS_PER_TRIP = 4
UNIT_CHUNKS = 4
S5_L = 8
LANES = 128
S5_QG = LANES // S5_CH
S5_NQ = D_S5 // LANES
S5_W = S5_L * LANES
S5_SW = S5_QG * 2 * S5_STATE
S5_SROWS = S5_NQ * S5_SW // LANES
VMEM_LIMIT = 56 * 1024 * 1024


def _split_bf16(x, n):
    parts = []
    rem = x
    for i in range(n):
        p = rem.astype(BF16)
        parts.append(p)
        if i + 1 < n:
            rem = rem - p.astype(F32)
    return parts


def _dot(a, b):
    return jnp.dot(a, b, preferred_element_type=F32)


def _dot_nt(a, b):
    return lax.dot_general(a, b, (((1,), (1,)), ((), ())), preferred_element_type=F32)


def _dot_split_lhs(x, rhs_bf16, n):
    acc = None
    for p in _split_bf16(x, n):
        d = _dot(p, rhs_bf16)
        acc = d if acc is None else acc + d
    return acc


def _sigmoid(x):
    return 1.0 / (1.0 + jnp.exp(-x))


def _ada_kernel(c_ref, w_ref, b_ref, o_ref):
    c = c_ref[...]
    act = c * _sigmoid(c)
    o_ref[...] = jnp.dot(act, w_ref[...], preferred_element_type=F32,
                         precision=lax.Precision.HIGHEST) + b_ref[...]


def _ada(c, w_ada, b_ada):
    bsz, d = c.shape
    rows = 8
    c_pad = jnp.zeros((rows, d), F32).at[:bsz].set(c)
    n_out = w_ada.shape[1]
    out = pl.pallas_call(
        _ada_kernel,
        grid=(n_out // d,),
        in_specs=[
            pl.BlockSpec((rows, d), lambda j: (0, 0)),
            pl.BlockSpec((d, d), lambda j: (0, j)),
            pl.BlockSpec((1, d), lambda j: (0, j)),
        ],
        out_specs=pl.BlockSpec((rows, d), lambda j: (0, j)),
        out_shape=jax.ShapeDtypeStruct((rows, n_out), F32),
        compiler_params=pltpu.CompilerParams(dimension_semantics=("arbitrary",)),
    )(c_pad, w_ada, b_ada.reshape(1, n_out))
    return out[:bsz]


def _inproj_kernel(x_ref, sh_ref, sc_ref, win_ref, mu_ref, w0_ref, w2_ref, a0_ref, a2_ref, g2_ref,
                   kk_ref, ka_ref, bd_ref,
                   r_out, k_out, v_out, ld_out, a_out, b_out, g_out, u_out, carry_ref, u_scr):
    @pl.when(pl.program_id(1) == 0)
    def _():
        carry_ref[...] = jnp.zeros_like(carry_ref)

    sub = INPROJ_SUB
    prev_last = carry_ref[...]
    for s in range(x_ref.shape[0] // sub):
        rows = slice(s * sub, (s + 1) * sub)
        crow = slice(s * (sub // S5_L), (s + 1) * (sub // S5_L))
        prev_last = _inproj_rows(x_ref[rows, :], prev_last, rows, crow, sh_ref, sc_ref, win_ref, mu_ref,
                                 w0_ref, w2_ref, a0_ref, a2_ref, g2_ref, kk_ref, ka_ref, bd_ref,
                                 r_out, k_out, v_out, ld_out, a_out, b_out, g_out, u_out, u_scr)
    carry_ref[...] = prev_last


def _inproj_rows(x, prev_last, rows, crow, sh_ref, sc_ref, win_ref, mu_ref, w0_ref, w2_ref, a0_ref, a2_ref,
                 g2_ref, kk_ref, ka_ref, bd_ref, r_out, k_out, v_out, ld_out, a_out, b_out, g_out, u_out, u_scr):
    tm = x.shape[0]
    ms = jnp.mean(x * x, axis=-1, keepdims=True)
    h = x * lax.rsqrt(ms + NORM_EPS)
    h = h * (1.0 + sc_ref[...]) + sh_ref[...]
    proj = _dot(h.astype(BF16), win_ref[...])
    z = proj[:, :D_SHIFT]
    for k in range(S5_NQ):
        u_scr[k, rows, :] = proj[:, D_SHIFT + k * LANES:D_SHIFT + (k + 1) * LANES]
    for tl in range(S5_L):
        for k in range(S5_NQ):
            lo = tl * D_S5 + k * LANES
            picked = u_scr[k, pl.ds(rows.start + tl, tm // S5_L, stride=S5_L), :]
            u_out[crow, lo:lo + LANES] = picked.astype(u_out.dtype)

    z_roll = pltpu.roll(z, 1, axis=0)
    row = lax.broadcasted_iota(jnp.int32, z.shape, 0)
    z_prev = jnp.where(row == 0, prev_last, z_roll)
    zz = z + mu_ref[...] * (z_prev - z)

    r = zz[:, 0:D_RWKV]
    k = zz[:, D_RWKV:2 * D_RWKV]
    v = zz[:, 2 * D_RWKV:3 * D_RWKV]
    o = 3 * D_RWKV
    w_lo = zz[:, o:o + LORA_W]
    a_lo = zz[:, o + LORA_W:o + LORA_W + LORA_A]
    g_lo = zz[:, o + LORA_W + LORA_A:o + LORA_W + LORA_A + LORA_G]

    wl = w0_ref[...] + _dot(jnp.tanh(w_lo).astype(BF16), w2_ref[...])
    ld_out[rows, :] = -math.exp(-0.5) * _sigmoid(wl)
    a = _sigmoid(a0_ref[...] + _dot(a_lo.astype(BF16), a2_ref[...]))
    g_out[rows, :] = _dot(_sigmoid(g_lo).astype(BF16), g2_ref[...]).astype(g_out.dtype)

    kk = k * kk_ref[...]
    n2 = _dot((kk * kk).astype(BF16), bd_ref[...])
    kk = kk / jnp.maximum(jnp.sqrt(n2), 1e-12)
    r_out[rows, :] = r.astype(r_out.dtype)
    k_out[rows, :] = (k * (1.0 + (a - 1.0) * ka_ref[...])).astype(k_out.dtype)
    v_out[rows, :] = v.astype(v_out.dtype)
    a_out[rows, :] = (-kk).astype(a_out.dtype)
    b_out[rows, :] = (kk * a).astype(b_out.dtype)
    return z[tm - 1:tm, :]


def _head_block_diag(n, dtype, value=1.0):
    i = jnp.arange(n) // HEAD
    return jnp.where(i[:, None] == i[None, :], value, 0.0).astype(dtype)


def _inproj(x, sh_m, sc_m, w_in, mu, w0, w2, a0, a2, g2, k_k, k_a, tm):
    bsz, seq, d = x.shape
    row = lambda a: a.reshape(1, -1)
    const = lambda shape: pl.BlockSpec(shape, lambda b, t: (0,) * len(shape), pipeline_mode=pl.Buffered(1))
    tok = lambda n: pl.BlockSpec((None, tm, n), lambda b, t: (b, t, 0))
    per_b = pl.BlockSpec((None, 1, d), lambda b, t: (b, 0, 0))
    out_sds = lambda dt: jax.ShapeDtypeStruct((bsz, seq, D_RWKV), dt)
    out_dtypes = [BF16, BF16, BF16, F32, BF16, BF16, BF16]
    return pl.pallas_call(
        _inproj_kernel,
        grid=(bsz, seq // tm),
        in_specs=[tok(d), per_b, per_b, const((d, D_IN)), const((1, D_SHIFT)),
                  const((1, D_RWKV)), const((LORA_W, D_RWKV)), const((1, D_RWKV)),
                  const((LORA_A, D_RWKV)), const((LORA_G, D_RWKV)), const((1, D_RWKV)),
                  const((1, D_RWKV)), const((D_RWKV, D_RWKV))],
        out_specs=[tok(D_RWKV)] * 7 + [pl.BlockSpec((None, tm // S5_L, S5_L * D_S5), lambda b, t: (b, t, 0))],
        out_shape=[out_sds(dt) for dt in out_dtypes]
        + [jax.ShapeDtypeStruct((bsz, seq // S5_L, S5_L * D_S5), BF16)],
        scratch_shapes=[pltpu.VMEM((1, D_SHIFT), F32), pltpu.VMEM((S5_NQ, tm, LANES), F32)],
        compiler_params=pltpu.CompilerParams(dimension_semantics=("parallel", "arbitrary"),
                                             vmem_limit_bytes=VMEM_LIMIT),
    )(x, sh_m.reshape(bsz, 1, d), sc_m.reshape(bsz, 1, d), w_in.astype(BF16), row(mu),
      row(w0), w2.astype(BF16), row(a0), a2.astype(BF16), g2.astype(BF16), row(k_k), row(k_a),
      _head_block_diag(D_RWKV, BF16))


def _per_head(yb, lo, hi):
    return jnp.concatenate([yb * lo, yb * hi], axis=0)


def _rwkv_prep(slabs, rk_rows, cst):
    row, strict, incl, bd, eye, m0, m1, m0w, m1w, eye_c, merge_masks = cst
    c = CHUNK
    r, k, v, ld, a, b = ([s[i] for s in slabs] for i in range(6))
    cum = ld
    sft = 1
    while sft < c:
        if sft < 8:
            cum = [x + jnp.where(row >= sft, pltpu.roll(x, sft, axis=0), 0.0) for x in cum]
        else:
            pad = jnp.zeros((sft, PAIR), F32)
            cum = [x + jnp.concatenate([pad, x[:c - sft]], axis=0) for x in cum]
        sft *= 2
    cl = [x[c - 1:c, :] for x in cum]
    w_to = [jnp.exp(x) for x in cum]
    w_inv = [jnp.exp(-x) for x in cum]
    w_prev = [jnp.exp(x - y) for x, y in zip(cum, ld)]
    w_rem = [jnp.exp(y - x) for x, y in zip(cum, cl)]
    w_all = [jnp.exp(y) for y in cl]
    rt = [x * w for x, w in zip(r, w_to)]
    kt = [x * w for x, w in zip(k, w_inv)]
    at = [x * w for x, w in zip(a, w_prev)]
    bt = [x * w for x, w in zip(b, w_inv)]
    bh = [x * w for x, w in zip(b, w_rem)]
    kh = [x * w for x, w in zip(k, w_rem)]
    lhs = [jnp.concatenate([x, y], axis=0).astype(BF16) for x, y in zip(at, rt)]
    rhs = [jnp.concatenate([_per_head(x.astype(BF16), m0, m1), _per_head(y.astype(BF16), m0, m1)], axis=0)
           for x, y in zip(bt, kt)]
    vb = [x.astype(BF16) for x in v]
    bk_t = [jnp.concatenate([x, y], axis=0).T.astype(BF16) for x, y in zip(bh, kh)]
    rkb = [(x * y * z).astype(BF16) for x, y, z in zip(r, k, rk_rows)]
    return list(zip(lhs, rhs, vb, at, rt, bk_t, w_all, rkb))


def _rwkv_chunk_maps(ops, cst):
    row, strict, incl, bd, eye, m0, m1, m0w, m1w, eye_c, merge_masks = cst
    c = CHUNK
    lhs, rhs, vb, at, rt, bk_t, w_all, _ = ([o[i] for o in ops] for i in range(8))
    per_head = _per_head
    a_all = [_dot_nt(x, y) for x, y in zip(lhs, rhs)]
    n_ab = [jnp.where(strict, x[:c, :PAIR], 0.0) for x in a_all]
    a_ak = [jnp.where(strict, x[:c, PAIR:], 0.0) for x in a_all]
    a_rb = [jnp.where(incl, x[c:, :PAIR], 0.0) for x in a_all]
    a_rk = [jnp.where(incl, x[c:, PAIR:], 0.0) for x in a_all]

    vs = [per_head(x, m0, m1) for x in vb]
    x0 = [jnp.concatenate([y, _dot(z.astype(BF16), w)], axis=1) for y, z, w in zip(at, a_ak, vs)]

    nb = [x.astype(BF16) for x in n_ab]
    t_inv = [eye_c + x * merge_masks[0] for x in n_ab]
    anchors = [None] * len(ops)
    for lvl, msk in enumerate(merge_masks[1:]):
        sz = 2 << lvl
        tb = [t.astype(BF16) for t in t_inv]
        if sz % 16:
            tmp = [_dot(t, per_head(n * msk, m0, m1)) for t, n in zip(tb, nb)]
            t_inv = [t + _dot(m.astype(BF16), per_head(h, m0, m1)) for t, m, h in zip(t_inv, tmp, tb)]
        else:
            low = [lo for lo in range(0, c, sz) if (lo // sz) % 2]
            pick = lambda y: jnp.concatenate([y[lo:lo + sz] for lo in low], axis=0)
            tmp = [_dot(pick(t), per_head(n * msk, m0, m1)) for t, n in zip(tb, nb)]
            upd = [_dot(m.astype(BF16), per_head(h, m0, m1)) for m, h in zip(tmp, tb)]

            def put_back(t, u):
                blocks = [t[lo:lo + sz] for lo in range(0, c, sz)]
                for j, lo in enumerate(low):
                    blocks[lo // sz] = blocks[lo // sz] + u[j * sz:(j + 1) * sz]
                return jnp.concatenate(blocks, axis=0)

            t_inv = [put_back(t, u) for t, u in zip(t_inv, upd)]
        for n in range(len(ops)):
            if lvl == (n * (len(merge_masks) - 1)) // len(ops):
                anchors[n] = tmp[n][:8]
    x1 = [_dot(t.astype(BF16), per_head(y.astype(BF16), m0w, m1w)) for t, y in zip(t_inv, x0)]
    x1b = [x.astype(BF16) for x in x1]
    rx = [_dot(z.astype(BF16), per_head(y, m0w, m1w)) for z, y in zip(a_rb, x1b)]
    r_hat = [x + y[:, :PAIR] for x, y in zip(rt, rx)]
    y_hat = [y[:, PAIR:] + _dot(z.astype(BF16), w) for y, z, w in zip(rx, a_rk, vs)]
    zero = jnp.zeros((c, PAIR), BF16)
    gm = [_dot(x, jnp.concatenate([y, jnp.concatenate([zero, z], axis=1)], axis=0))
          for x, y, z in zip(bk_t, x1b, vb)]
    m_mat = [eye * w + x[:, :PAIR] * bd for w, x in zip(w_all, gm)]
    s_hat = [x[:, PAIR:] * bd for x in gm]
    return list(zip(r_hat, y_hat, m_mat, s_hat)), anchors


def _rwkv_kernel(r_ref, k_ref, v_ref, ld_ref, a_ref, b_ref, g_ref, rk_ref, lw_ref, lb_ref,
                 bd_ref, avg2_ref, stat2_ref, y_ref, s_ref, *prep_refs):
    t = pl.program_id(1)

    @pl.when(t == 0)
    def _():
        s_ref[...] = jnp.zeros_like(s_ref)

    c = CHUNK
    lane = lax.broadcasted_iota(jnp.int32, (1, PAIR), 1)
    m0 = (lane < HEAD).astype(F32).astype(BF16)
    m1 = (lane >= HEAD).astype(F32).astype(BF16)
    lane_w = lax.broadcasted_iota(jnp.int32, (1, 2 * PAIR), 1) % PAIR
    m0w = (lane_w < HEAD).astype(F32).astype(BF16)
    m1w = (lane_w >= HEAD).astype(F32).astype(BF16)
    row = lax.broadcasted_iota(jnp.int32, (c, PAIR), 0)
    col = lax.broadcasted_iota(jnp.int32, (c, PAIR), 1) % c
    strict = col < row
    incl = col <= row
    bd = bd_ref[...]
    ri = lax.broadcasted_iota(jnp.int32, (PAIR, PAIR), 0)
    ci = lax.broadcasted_iota(jnp.int32, (PAIR, PAIR), 1)
    eye = (ri == ci).astype(F32)
    eye_c = (col == row).astype(F32)
    merge_masks = []
    sz = 1
    while sz < c:
        msk = (row // (2 * sz) == col // (2 * sz)) & (row % (2 * sz) >= sz) & (col % (2 * sz) < sz)
        merge_masks.append(msk.astype(F32) if sz == 1 else msk.astype(F32).astype(BF16))
        sz *= 2
    cst = (row, strict, incl, bd, eye, m0, m1, m0w, m1w, eye_c, merge_masks)

    n_pairs = D_RWKV // PAIR
    span = UNIT_CHUNKS * c
    units = [(ch, p) for ch in range(UNIT_CHUNKS) for p in range(n_pairs)]
    lanes = [slice(p * PAIR, (p + 1) * PAIR) for _, p in units]
    n_blocks = r_ref.shape[0] // span
    slot_a, slot_b = prep_refs[:len(prep_refs) // 2], prep_refs[len(prep_refs) // 2:]

    def unit_rows(blk):
        base = pl.multiple_of(blk * span, span)
        return [pl.ds(base + ch * c, c) for ch, _ in units]

    def exact_zero(x):
        bits = pltpu.bitcast(x, jnp.uint32)
        half = jnp.uint32(16)
        return pltpu.bitcast(lax.shift_right_logical(lax.shift_right_logical(bits, half), half), F32)

    def prep_block(blk, slot, after=None):
        slabs = [tuple(ref[rs, ls].astype(F32) for ref in (r_ref, k_ref, v_ref, ld_ref, a_ref, b_ref))
                 for rs, ls in zip(unit_rows(blk), lanes)]
        if after is not None:
            tied = [jnp.concatenate([s[3][:8] + exact_zero(x), s[3][8:]], axis=0) for s, x in zip(slabs, after)]
            slabs = [s[:3] + (ld,) + s[4:] for s, ld in zip(slabs, tied)]
        ops = _rwkv_prep(slabs, [rk_ref[:, ls] for ls in lanes], cst)
        for n, op in enumerate(ops):
            for ref, val in zip(slot, op):
                ref[n] = val

    def main_block(blk, slot):
        ops = [tuple(ref[n] for ref in slot) for n in range(len(units))]
        maps, anchors = _rwkv_chunk_maps(ops, cst)

        state = [s_ref[p] for p in range(n_pairs)]
        ys = []
        for (ch, p), (r_hat, y_hat, m_mat, s_hat) in zip(units, maps):
            sb = state[p].astype(BF16)
            ym = _dot(jnp.concatenate([r_hat, m_mat], axis=0).astype(BF16), sb)
            ys.append(ym[:c] + y_hat)
            state[p] = ym[c:] + s_hat
        for p in range(n_pairs):
            s_ref[p] = state[p]

        y_all = jnp.concatenate(ys, axis=0)
        dlt = y_all - _dot(jnp.concatenate(_split_bf16(y_all, 2), axis=1), avg2_ref[...])
        rkb = jnp.concatenate([op[7] for op in ops], axis=0)
        stats = _dot(jnp.concatenate([(dlt * dlt).astype(BF16), rkb], axis=1), stat2_ref[...])
        yn = dlt * lax.rsqrt(stats[:, :PAIR] + LNX_EPS)
        bonus = stats[:, PAIR:]
        for n, (op, rs, ls) in enumerate(zip(ops, unit_rows(blk), lanes)):
            un = slice(n * c, (n + 1) * c)
            y = yn[un] * lw_ref[:, ls] + lb_ref[:, ls] + bonus[un] * op[2].astype(F32)
            y_ref[rs, ls] = (y * g_ref[rs, ls].astype(F32)).astype(y_ref.dtype)
        return anchors

    prep_block(0, slot_a)

    per_trip = RWKV_BLOCKS_PER_TRIP

    def body(j, carry):
        for i in range(0, per_trip, 2):
            first = per_trip * j + i
            done = main_block(first, slot_a)
            prep_block(first + 1, slot_b, after=done)
            done = main_block(first + 1, slot_b)
            prep_block(jnp.minimum(first + 2, n_blocks - 1), slot_a, after=done)
        return carry

    lax.fori_loop(0, n_blocks // per_trip, body, 0)


def _rwkv(r, k, v, ld, a, b, g, r_k, lnx_w, lnx_b, tb):
    bsz, seq, d = r.shape
    tok = pl.BlockSpec((None, tb, d), lambda i, t: (i, t, 0))
    const = lambda shape: pl.BlockSpec(shape, lambda i, t: (0,) * len(shape))
    assert (tb // (UNIT_CHUNKS * CHUNK)) % 2 == 0, "the block pipeline is written for an even block count"
    c, nu = CHUNK, UNIT_CHUNKS * (d // PAIR)
    slot = [pltpu.VMEM((nu, 2 * c, PAIR), BF16), pltpu.VMEM((nu, 4 * c, PAIR), BF16),
            pltpu.VMEM((nu, c, PAIR), BF16), pltpu.VMEM((nu, c, PAIR), F32), pltpu.VMEM((nu, c, PAIR), F32),
            pltpu.VMEM((nu, PAIR, 2 * c), BF16), pltpu.VMEM((nu, 1, PAIR), F32), pltpu.VMEM((nu, c, PAIR), BF16)]
    avg = _head_block_diag(PAIR, BF16, 1.0 / HEAD)
    ones = _head_block_diag(PAIR, BF16)
    zero = jnp.zeros((PAIR, PAIR), BF16)
    avg2 = jnp.concatenate([avg, avg], axis=0)
    stat2 = jnp.concatenate([jnp.concatenate([avg, zero], axis=1), jnp.concatenate([zero, ones], axis=1)], axis=0)
    return pl.pallas_call(
        _rwkv_kernel,
        grid=(bsz, seq // tb),
        in_specs=[tok] * 7 + [const((1, d))] * 3
        + [const((PAIR, PAIR)), const((2 * PAIR, PAIR)), const((2 * PAIR, 2 * PAIR))],
        out_specs=tok,
        out_shape=jax.ShapeDtypeStruct((bsz, seq, d), BF16),
        scratch_shapes=[pltpu.VMEM((d // PAIR, PAIR, PAIR), F32)] + slot + slot,
        compiler_params=pltpu.CompilerParams(dimension_semantics=("parallel", "arbitrary"),
                                             vmem_limit_bytes=VMEM_LIMIT),
    )(r, k, v, ld, a, b, g, r_k.reshape(1, d), lnx_w.reshape(1, d), lnx_b.reshape(1, d),
      _head_block_diag(PAIR, F32), avg2, stat2)


def _s5_prep_kernel(are_ref, aim_ref, ldt_ref, btre_ref, btim_ref, cre_ref, cim_ref, dsk_ref,
                    qt_out, p_out, t_out, lam_out):
    for gi in range(are_ref.shape[0]):
        _s5_prep_group(are_ref[gi], aim_ref[gi], ldt_ref[gi], btre_ref[gi], btim_ref[gi], cre_ref[gi],
                       cim_ref[gi], dsk_ref[gi], qt_out.at[gi], p_out.at[gi], t_out.at[gi], lam_out.at[gi])


def _s5_prep_group(a_re, a_im, log_dt, bt_re, bt_im, c_re, c_im, d_tile, qt_out, p_out, t_out, lam_out):
    dt = jnp.exp(log_dt)
    nd = 24
    dpow = lax.broadcasted_iota(jnp.int32, (nd, S5_STATE), 0).astype(F32)
    mag = jnp.exp(dpow * (dt * a_re))
    ang = dpow * (dt * a_im)
    e_re = mag * jnp.cos(ang)
    e_im = mag * jnp.sin(ang)
    lam_re = e_re[1:2, :]
    lam_im = e_im[1:2, :]
    den = a_re * a_re + a_im * a_im
    pp = lam_re - 1.0
    qq = lam_im
    coef_re = (pp * a_re + qq * a_im) / den
    coef_im = (qq * a_re - pp * a_im) / den
    bb_re = coef_re * bt_re - coef_im * bt_im
    bb_im = coef_re * bt_im + coef_im * bt_re
    c_lam = []
    for d in range(S5_L + 1):
        er = e_re[d:d + 1, :]
        ei = e_im[d:d + 1, :]
        c_lam.append(jnp.concatenate([c_re * er - c_im * ei, -(c_re * ei + c_im * er)], axis=1))
    qt_out[...] = jnp.concatenate(c_lam[1:], axis=0)
    for j in range(S5_L):
        d = S5_L - 1 - j
        er = e_re[d:d + 1, :]
        ei = e_im[d:d + 1, :]
        p_out[j * S5_CH:(j + 1) * S5_CH, :] = jnp.concatenate(
            [er * bb_re - ei * bb_im, er * bb_im + ei * bb_re], axis=1)
    y0 = jnp.concatenate([bb_re, bb_im], axis=1)
    t_rows = []
    for j in range(S5_L):
        lagged = jnp.concatenate([jnp.zeros_like(c_lam[0])] * j + c_lam[:S5_L - j], axis=0)
        t_rows.append(lax.dot_general(y0, lagged, (((1,), (1,)), ((), ())), preferred_element_type=F32,
                                      precision=lax.Precision.HIGHEST))
    gw = S5_L * S5_CH
    on_diag = lax.broadcasted_iota(jnp.int32, (gw, gw), 0) == lax.broadcasted_iota(jnp.int32, (gw, gw), 1)
    t_out[...] = jnp.concatenate(t_rows, axis=0) + jnp.where(on_diag, d_tile, 0.0)
    lam_out[...] = jnp.concatenate([e_re[S5_L:S5_L + 1, :], e_im[S5_L:S5_L + 1, :]], axis=1)


def _s5_prep(a_re, a_im, log_dt, b_re, b_im, c_re, c_im, d_skip):
    g, p = a_re.shape
    per_g = lambda shape: pl.BlockSpec((S5_QG,) + shape, lambda i: (i,) + (0,) * len(shape))
    gw = S5_L * S5_CH
    return pl.pallas_call(
        _s5_prep_kernel,
        grid=(g // S5_QG,),
        in_specs=[per_g((1, p)), per_g((1, p)), per_g((1, 1)), per_g((S5_CH, p)), per_g((S5_CH, p)),
                  per_g((S5_CH, p)), per_g((S5_CH, p)), per_g((1, gw))],
        out_specs=[per_g((gw, 2 * p)), per_g((gw, 2 * p)), per_g((gw, gw)), per_g((1, 2 * p))],
        out_shape=[jax.ShapeDtypeStruct((g, gw, 2 * p), F32),
                   jax.ShapeDtypeStruct((g, gw, 2 * p), F32),
                   jax.ShapeDtypeStruct((g, gw, gw), F32),
                   jax.ShapeDtypeStruct((g, 1, 2 * p), F32)],
        compiler_params=pltpu.CompilerParams(dimension_semantics=("arbitrary",)),
    )(a_re.reshape(g, 1, p), a_im.reshape(g, 1, p), log_dt.reshape(g, 1, 1),
      jnp.swapaxes(b_re, 1, 2), jnp.swapaxes(b_im, 1, 2), c_re, c_im,
      jnp.tile(d_skip, (1, S5_L)).reshape(g, 1, gw))


def _s5_weights_kernel(p_ref, t_ref, qt_ref, perm_ref, bp_out, bt_out, bq_out, rows_ok_ref):
    gw = p_ref.shape[1]
    for h in range(p_ref.shape[0]):
        cols = slice(h * gw, (h + 1) * gw)
        perm_h = perm_ref[:, cols]
        bp_out[:, cols] = _dot(perm_h, p_ref[h].astype(BF16)).astype(BF16)
        rows_ok_ref[:, cols] = _dot(perm_h, t_ref[h].astype(BF16)).astype(BF16)
        bq_out[cols, :] = _dot_nt(qt_ref[h].T.astype(BF16), perm_h).astype(BF16)
    bt_out[...] = _dot_nt(rows_ok_ref[...], perm_ref[...]).astype(BF16)


def _s5_weights(p_all, t_all, qt_all):
    g, gw, sw = p_all.shape
    dst = jnp.arange(S5_W)
    j, h, a = dst // LANES, (dst % LANES) // S5_CH, dst % S5_CH
    src = h * gw + j * S5_CH + a
    perm = (src[:, None] == jnp.arange(S5_W)[None, :]).astype(BF16)
    grp = lambda n: pl.BlockSpec((S5_QG, gw, n), lambda q: (q, 0, 0))
    big = lambda r, c: pl.BlockSpec((None, r, c), lambda q: (q, 0, 0))
    return pl.pallas_call(
        _s5_weights_kernel,
        grid=(S5_NQ,),
        in_specs=[grp(sw), grp(gw), grp(sw), pl.BlockSpec((S5_W, S5_W), lambda q: (0, 0))],
        out_specs=[big(S5_W, S5_SW), big(S5_W, S5_W), big(S5_SW, S5_W)],
        out_shape=[jax.ShapeDtypeStruct((S5_NQ, S5_W, S5_SW), BF16),
                   jax.ShapeDtypeStruct((S5_NQ, S5_W, S5_W), BF16),
                   jax.ShapeDtypeStruct((S5_NQ, S5_SW, S5_W), BF16)],
        scratch_shapes=[pltpu.VMEM((S5_W, S5_W), BF16)],
        compiler_params=pltpu.CompilerParams(dimension_semantics=("arbitrary",),
                                             vmem_limit_bytes=VMEM_LIMIT),
    )(p_all, t_all, qt_all, perm)


def _s5_kernel(u_ref, bp_ref, bt_ref, bq_ref, lr_ref, li_ref, y_ref, z_ref, s_ref):
    @pl.when(pl.program_id(0) == 0)
    def _():
        s_ref[...] = jnp.zeros_like(s_ref)

    nb, ct, _ = u_ref.shape
    rows = nb * ct
    kb = S5_SW // LANES

    def cols(q):
        return [slice(tl * D_S5 + q * LANES, tl * D_S5 + (q + 1) * LANES) for tl in range(S5_L)]

    def x_of(q):
        return jnp.concatenate([u_ref[:, :, cs].reshape(rows, LANES) for cs in cols(q)], axis=1)

    for q in range(S5_NQ):
        z = _dot(x_of(q), bp_ref[q])
        z_ref[:, q * kb:(q + 1) * kb, :] = z.reshape(rows, kb, LANES)

    for q in range(S5_NQ):
        gs = slice(q * kb, (q + 1) * kb)
        pick = lambda ref: jnp.concatenate([ref[b * S5_SROWS + q * kb:b * S5_SROWS + (q + 1) * kb] for b in range(nb)], axis=0)
        lr, li = pick(lr_ref), pick(li_ref)
        s = pick(s_ref)
        s_sw = pltpu.roll(s, S5_STATE, axis=1)
        for c in range(ct):
            z = jnp.concatenate([z_ref[b * ct + c, gs, :] for b in range(nb)], axis=0)
            for b in range(nb):
                z_ref[b * ct + c, gs, :] = s[b * kb:(b + 1) * kb]
            z_sw = pltpu.roll(z, S5_STATE, axis=1)
            s, s_sw = lr * s + li * s_sw + z, lr * s_sw - li * s + z_sw
        for b in range(nb):
            s_ref[b * S5_SROWS + q * kb:b * S5_SROWS + (q + 1) * kb] = s[b * kb:(b + 1) * kb]

    for q in range(S5_NQ):
        start = z_ref[:, q * kb:(q + 1) * kb, :].reshape(rows, kb * LANES).astype(BF16)
        x = x_of(q)
        pair = 2 * LANES
        local = jnp.concatenate([_dot(x[:, :(i + 1) * pair], bt_ref[q, :(i + 1) * pair, i * pair:(i + 1) * pair])
                                 for i in range(S5_W // pair)], axis=1)
        y = (local + _dot(start, bq_ref[q])).astype(y_ref.dtype)
        for tl, cs in enumerate(cols(q)):
            y_ref[:, :, cs] = y[:, tl * LANES:(tl + 1) * LANES].reshape(nb, ct, LANES)


def _s5_core(u2, a_re, a_im, log_dt, b_re, b_im, c_re, c_im, d_skip, ct):
    bsz, nck, _ = u2.shape
    p, ll, nq = S5_STATE, S5_L, S5_NQ
    qt_all, p_all, t_all, lam = _s5_prep(a_re, a_im, log_dt, b_re, b_im, c_re, c_im, d_skip)
    big_p, big_t, big_q = _s5_weights(p_all, t_all, qt_all)
    lam_re = lam[:, 0, :p]
    lam_im = lam[:, 0, p:]
    lr = jnp.tile(jnp.concatenate([lam_re, lam_re], axis=1), (bsz, 1))
    li = jnp.tile(jnp.concatenate([-lam_im, lam_im], axis=1), (bsz, 1))

    once = dict(pipeline_mode=pl.Buffered(1))
    const = lambda shape: pl.BlockSpec(shape, lambda i: (0,) * len(shape), **once)
    tok = pl.BlockSpec((bsz, ct, ll * D_S5), lambda i: (0, i, 0))
    y2 = pl.pallas_call(
        _s5_kernel,
        grid=(nck // ct,),
        in_specs=[tok, const((nq, S5_W, S5_SW)), const((nq, S5_W, S5_W)), const((nq, S5_SW, S5_W)),
                  const((bsz * S5_SROWS, LANES)), const((bsz * S5_SROWS, LANES))],
        out_specs=tok,
        out_shape=jax.ShapeDtypeStruct((bsz, nck, ll * D_S5), BF16),
        scratch_shapes=[pltpu.VMEM((bsz * ct, S5_SROWS, LANES), F32),
                        pltpu.VMEM((bsz * S5_SROWS, LANES), F32)],
        compiler_params=pltpu.CompilerParams(dimension_semantics=("arbitrary",),
                                             vmem_limit_bytes=VMEM_LIMIT),
    )(u2, big_p, big_t, big_q, lr, li)
    return y2


def _tail_kernel(x_ref, yr_ref, ys_ref, gm_ref, shf_ref, scf_ref, gf_ref, wglu_ref, bglu_ref, gain_ref,
                 wo_ref, wg_ref, wu_ref, wd_ref, fg_ref, o_ref, ys_scr, act_scr, *, ff_tile):
    tm = x_ref.shape[0]
    for tl in range(S5_L):
        for k in range(S5_NQ):
            lo = tl * D_S5 + k * LANES
            ys_scr[k, pl.ds(tl, tm // S5_L, stride=S5_L), :] = ys_ref[:, lo:lo + LANES].astype(F32)
    ys = jnp.concatenate([ys_scr[k] for k in range(S5_NQ)], axis=1)
    zz = 0.5 * ys * (1.0 + jnp.tanh(math.sqrt(2.0 / math.pi) * (ys + 0.044715 * (ys * ys * ys))))
    gl = zz * _sigmoid(_dot(zz.astype(BF16), wglu_ref[...]) + bglu_ref[...])
    gl = gl * lax.rsqrt(jnp.mean(gl * gl, axis=-1, keepdims=True) + NORM_EPS) * gain_ref[...]
    mix = _dot(yr_ref[...].astype(BF16), wo_ref[:D_RWKV, :]) + _dot(gl.astype(BF16), wo_ref[D_RWKV:, :])
    x1 = x_ref[...] + gm_ref[...] * mix
    h = x1 * lax.rsqrt(jnp.mean(x1 * x1, axis=-1, keepdims=True) + NORM_EPS)
    h = (h * (1.0 + scf_ref[...]) + shf_ref[...]).astype(BF16)
    for j in range(D_FF // ff_tile):
        cs = slice(j * ff_tile, (j + 1) * ff_tile)
        gate = _dot(h, wg_ref[:, cs])
        up = _dot(h, wu_ref[:, cs])
        act_scr[:, cs] = (gate * _sigmoid(gate) * up).astype(BF16)
    x2 = x1 + gf_ref[...] * _dot(act_scr[...], wd_ref[...])
    o_ref[...] = x2 * lax.rsqrt(jnp.mean(x2 * x2, axis=-1, keepdims=True) + NORM_EPS) * fg_ref[...]


def _tail(x, y_rwkv, y_s5, g_m, sh_f, sc_f, g_f, w_glu, b_glu, gain, w_out, w_gate, w_up, w_down,
          final_gain, tm, ff_tile):
    bsz, seq, d = x.shape
    once = dict(pipeline_mode=pl.Buffered(1))
    const = lambda shape: pl.BlockSpec(shape, lambda b, t: (0,) * len(shape), **once)
    tok = lambda n: pl.BlockSpec((None, tm, n), lambda b, t: (b, t, 0))
    per_b = pl.BlockSpec((None, 1, d), lambda b, t: (b, 0, 0))
    b3 = lambda a: a.reshape(bsz, 1, d)
    return pl.pallas_call(
        functools.partial(_tail_kernel, ff_tile=ff_tile),
        grid=(bsz, seq // tm),
        in_specs=[tok(d), tok(D_RWKV), pl.BlockSpec((None, tm // S5_L, S5_L * D_S5), lambda b, t: (b, t, 0)),
                  per_b, per_b, per_b, per_b,
                  const((D_S5, D_S5)), const((1, D_S5)), const((1, D_S5)),
                  const((D_RWKV + D_S5, d)), const((d, D_FF)), const((d, D_FF)), const((D_FF, d)),
                  const((1, d))],
        out_specs=tok(d),
        out_shape=jax.ShapeDtypeStruct((bsz, seq, d), F32),
        scratch_shapes=[pltpu.VMEM((S5_NQ, tm, LANES), F32), pltpu.VMEM((tm, D_FF), BF16)],
        compiler_params=pltpu.CompilerParams(dimension_semantics=("parallel", "parallel"),
                                             vmem_limit_bytes=VMEM_LIMIT),
    )(x, y_rwkv, y_s5, b3(g_m), b3(sh_f), b3(sc_f), b3(g_f), w_glu.astype(BF16), b_glu.reshape(1, -1),
      gain.reshape(1, -1), w_out.astype(BF16), w_gate.astype(BF16), w_up.astype(BF16),
      w_down.astype(BF16), final_gain.reshape(1, d))


def kernel(x, c, w_ada, b_ada, w_in, mu_shift, rw_w0, rw_w2, rw_a0, rw_a2, rw_g2, rw_k_k, rw_k_a, rw_r_k,
           rw_lnx_w, rw_lnx_b, s5_a_re, s5_a_im, s5_log_dt, s5_b_re, s5_b_im, s5_c_re, s5_c_im, s5_d,
           s5_w_glu, s5_b_glu, s5_gain, w_out, ffn_w_gate, ffn_w_up, ffn_w_down, final_gain):
    assert w_ada.shape[0] == 1, "single-layer trunk"
    seq = x.shape[1]
    ada = _ada(c, w_ada[0], b_ada[0])
    sh_m, sc_m, g_m, sh_f, sc_f, g_f = jnp.split(ada, 6, axis=-1)
    r, k, v, ld, a, b, g, u = _inproj(x, sh_m, sc_m, w_in[0], mu_shift[0], rw_w0[0], rw_w2[0], rw_a0[0],
                                      rw_a2[0], rw_g2[0], rw_k_k[0], rw_k_a[0], tm=min(512, seq))
    y_rwkv = _rwkv(r, k, v, ld, a, b, g, rw_r_k[0], rw_lnx_w[0], rw_lnx_b[0], tb=min(2048, seq))
    y_s5 = _s5_core(u, s5_a_re[0], s5_a_im[0], s5_log_dt[0], s5_b_re[0], s5_b_im[0], s5_c_re[0],
                    s5_c_im[0], s5_d[0], ct=min(128, seq // S5_L))
    return _tail(x, y_rwkv, y_s5, g_m, sh_f, sc_f, g_f, s5_w_glu[0], s5_b_glu[0], s5_gain[0], w_out[0],
                 ffn_w_gate[0], ffn_w_up[0], ffn_w_down[0], final_gain, tm=min(512, seq), ff_tile=256)
```

```python
import functools
import math

import jax
import jax.numpy as jnp
from jax import lax
from jax.experimental import pallas as pl
from jax.experimental.pallas import tpu as pltpu

F32 = jnp.float32
BF16 = jnp.bfloat16

D_MODEL = 1024
D_RWKV = 512
D_S5 = 512
HEAD = 64
LORA_W = 64
LORA_A = 64
LORA_G = 128
S5_CH = 16
S5_GROUPS = 32
S5_STATE = 64
D_FF = 2816
D_SHIFT = 3 * D_RWKV + LORA_W + LORA_A + LORA_G
D_IN = D_SHIFT + D_S5
NORM_EPS = 1e-6
LNX_EPS = 64e-5

CHUNK = 64
PAIR = 2 * HEAD
INPROJ_SUB = 128
RWKV_BLOCKS_PER_TRIP = 8
UNIT_CHUNKS = 4
S5_L = 8
LANES = 128
S5_QG = LANES // S5_CH
S5_NQ = D_S5 // LANES
S5_W = S5_L * LANES
S5_SW = S5_QG * 2 * S5_STATE
S5_SROWS = S5_NQ * S5_SW // LANES
VMEM_LIMIT = 56 * 1024 * 1024

INPROJ_TILE = 1024
RWKV_TILE = 2048
S5_TILE = 128
TAIL_TILE = 1024
FF_TILE = 256


def _split_bf16(x, n):
    parts = []
    rem = x
    for i in range(n):
        p = rem.astype(BF16)
        parts.append(p)
        if i + 1 < n:
            rem = rem - p.astype(F32)
    return parts


def _dot(a, b):
    return jnp.dot(a, b, preferred_element_type=F32)


def _dot_nt(a, b):
    return lax.dot_general(a, b, (((1,), (1,)), ((), ())), preferred_element_type=F32)


def _dot_split_lhs(x, rhs_bf16, n):
    acc = None
    for p in _split_bf16(x, n):
        d = _dot(p, rhs_bf16)
        acc = d if acc is None else acc + d
    return acc


def _sigmoid(x):
    return 1.0 / (1.0 + jnp.exp(-x))


def _ada_kernel(c_ref, w_ref, b_ref, o_ref):
    c = c_ref[...]
    act = c * _sigmoid(c)
    o_ref[...] = jnp.dot(act, w_ref[...], preferred_element_type=F32,
                         precision=lax.Precision.HIGHEST) + b_ref[...]


def _ada(c, w_ada, b_ada):
    bsz, d = c.shape
    rows = 8
    c_pad = jnp.zeros((rows, d), F32).at[:bsz].set(c)
    n_out = w_ada.shape[1]
    out = pl.pallas_call(
        _ada_kernel,
        grid=(n_out // d,),
        in_specs=[
            pl.BlockSpec((rows, d), lambda j: (0, 0)),
            pl.BlockSpec((d, d), lambda j: (0, j)),
            pl.BlockSpec((1, d), lambda j: (0, j)),
        ],
        out_specs=pl.BlockSpec((rows, d), lambda j: (0, j)),
        out_shape=jax.ShapeDtypeStruct((rows, n_out), F32),
        compiler_params=pltpu.CompilerParams(dimension_semantics=("arbitrary",)),
    )(c_pad, w_ada, b_ada.reshape(1, n_out))
    return out[:bsz]


def _inproj_kernel(x_ref, sh_ref, sc_ref, win_ref, mu_ref, w0_ref, w2_ref, a0_ref, a2_ref, g2_ref,
                   kk_ref, ka_ref, bd_ref,
                   r_out, k_out, v_out, ld_out, a_out, b_out, g_out, u_out, carry_ref, u_scr):
    @pl.when(pl.program_id(1) == 0)
    def _():
        carry_ref[...] = jnp.zeros_like(carry_ref)

    sub = INPROJ_SUB
    prev_last = carry_ref[...]
    for s in range(x_ref.shape[0] // sub):
        rows = slice(s * sub, (s + 1) * sub)
        crow = slice(s * (sub // S5_L), (s + 1) * (sub // S5_L))
        prev_last = _inproj_rows(x_ref[rows, :], prev_last, rows, crow, sh_ref, sc_ref, win_ref, mu_ref,
                                 w0_ref, w2_ref, a0_ref, a2_ref, g2_ref, kk_ref, ka_ref, bd_ref,
                                 r_out, k_out, v_out, ld_out, a_out, b_out, g_out, u_out, u_scr)
    carry_ref[...] = prev_last


def _inproj_rows(x, prev_last, rows, crow, sh_ref, sc_ref, win_ref, mu_ref, w0_ref, w2_ref, a0_ref, a2_ref,
                 g2_ref, kk_ref, ka_ref, bd_ref, r_out, k_out, v_out, ld_out, a_out, b_out, g_out, u_out, u_scr):
    tm = x.shape[0]
    ms = jnp.mean(x * x, axis=-1, keepdims=True)
    h = x * lax.rsqrt(ms + NORM_EPS)
    h = h * (1.0 + sc_ref[...]) + sh_ref[...]
    proj = _dot(h.astype(BF16), win_ref[...])
    z = proj[:, :D_SHIFT]
    for k in range(S5_NQ):
        u_scr[k, rows, :] = proj[:, D_SHIFT + k * LANES:D_SHIFT + (k + 1) * LANES]
    for tl in range(S5_L):
        for k in range(S5_NQ):
            lo = tl * D_S5 + k * LANES
            picked = u_scr[k, pl.ds(rows.start + tl, tm // S5_L, stride=S5_L), :]
            u_out[crow, lo:lo + LANES] = picked.astype(u_out.dtype)

    z_roll = pltpu.roll(z, 1, axis=0)
    row = lax.broadcasted_iota(jnp.int32, z.shape, 0)
    z_prev = jnp.where(row == 0, prev_last, z_roll)
    zz = z + mu_ref[...] * (z_prev - z)

    r = zz[:, 0:D_RWKV]
    k = zz[:, D_RWKV:2 * D_RWKV]
    v = zz[:, 2 * D_RWKV:3 * D_RWKV]
    o = 3 * D_RWKV
    w_lo = zz[:, o:o + LORA_W]
    a_lo = zz[:, o + LORA_W:o + LORA_W + LORA_A]
    g_lo = zz[:, o + LORA_W + LORA_A:o + LORA_W + LORA_A + LORA_G]

    wl = w0_ref[...] + _dot(jnp.tanh(w_lo).astype(BF16), w2_ref[...])
    ld_out[rows, :] = -math.exp(-0.5) * _sigmoid(wl)
    a = _sigmoid(a0_ref[...] + _dot(a_lo.astype(BF16), a2_ref[...]))
    g_out[rows, :] = _dot(_sigmoid(g_lo).astype(BF16), g2_ref[...]).astype(g_out.dtype)

    kk = k * kk_ref[...]
    n2 = _dot((kk * kk).astype(BF16), bd_ref[...])
    kk = kk / jnp.maximum(jnp.sqrt(n2), 1e-12)
    r_out[rows, :] = r.astype(r_out.dtype)
    k_out[rows, :] = (k * (1.0 + (a - 1.0) * ka_ref[...])).astype(k_out.dtype)
    v_out[rows, :] = v.astype(v_out.dtype)
    a_out[rows, :] = (-kk).astype(a_out.dtype)
    b_out[rows, :] = (kk * a).astype(b_out.dtype)
    return z[tm - 1:tm, :]


def _head_block_diag(n, dtype, value=1.0):
    i = jnp.arange(n) // HEAD
    return jnp.where(i[:, None] == i[None, :], value, 0.0).astype(dtype)


def _inproj(x, sh_m, sc_m, w_in, mu, w0, w2, a0, a2, g2, k_k, k_a, tm):
    bsz, seq, d = x.shape
    row = lambda a: a.reshape(1, -1)
    const = lambda shape: pl.BlockSpec(shape, lambda b, t: (0,) * len(shape), pipeline_mode=pl.Buffered(1))
    tok = lambda n: pl.BlockSpec((None, tm, n), lambda b, t: (b, t, 0))
    per_b = pl.BlockSpec((None, 1, d), lambda b, t: (b, 0, 0))
    out_sds = lambda dt: jax.ShapeDtypeStruct((bsz, seq, D_RWKV), dt)
    out_dtypes = [BF16, BF16, BF16, F32, BF16, BF16, BF16]
    return pl.pallas_call(
        _inproj_kernel,
        grid=(bsz, seq // tm),
        in_specs=[tok(d), per_b, per_b, const((d, D_IN)), const((1, D_SHIFT)),
                  const((1, D_RWKV)), const((LORA_W, D_RWKV)), const((1, D_RWKV)),
                  const((LORA_A, D_RWKV)), const((LORA_G, D_RWKV)), const((1, D_RWKV)),
                  const((1, D_RWKV)), const((D_RWKV, D_RWKV))],
        out_specs=[tok(D_RWKV)] * 7 + [pl.BlockSpec((None, tm // S5_L, S5_L * D_S5), lambda b, t: (b, t, 0))],
        out_shape=[out_sds(dt) for dt in out_dtypes]
        + [jax.ShapeDtypeStruct((bsz, seq // S5_L, S5_L * D_S5), BF16)],
        scratch_shapes=[pltpu.VMEM((1, D_SHIFT), F32), pltpu.VMEM((S5_NQ, tm, LANES), F32)],
        compiler_params=pltpu.CompilerParams(dimension_semantics=("parallel", "arbitrary"),
                                             vmem_limit_bytes=VMEM_LIMIT),
    )(x, sh_m.reshape(bsz, 1, d), sc_m.reshape(bsz, 1, d), w_in.astype(BF16), row(mu),
      row(w0), w2.astype(BF16), row(a0), a2.astype(BF16), g2.astype(BF16), row(k_k), row(k_a),
      _head_block_diag(D_RWKV, BF16))


def _per_head(yb, lo, hi):
    return jnp.concatenate([yb * lo, yb * hi], axis=0)


def _rwkv_prep(slabs, rk_rows, cst):
    row, strict, incl, bd, eye, m0, m1, m0w, m1w, eye_c, merge_masks = cst
    c = CHUNK
    r, k, v, ld, a, b = ([s[i] for s in slabs] for i in range(6))
    cum = ld
    sft = 1
    while sft < c:
        if sft < 8:
            cum = [x + jnp.where(row >= sft, pltpu.roll(x, sft, axis=0), 0.0) for x in cum]
        else:
            pad = jnp.zeros((sft, PAIR), F32)
            cum = [x + jnp.concatenate([pad, x[:c - sft]], axis=0) for x in cum]
        sft *= 2
    cl = [x[c - 1:c, :] for x in cum]
    w_to = [jnp.exp(x) for x in cum]
    w_inv = [jnp.exp(-x) for x in cum]
    w_prev = [jnp.exp(x - y) for x, y in zip(cum, ld)]
    w_rem = [jnp.exp(y - x) for x, y in zip(cum, cl)]
    w_all = [jnp.exp(y) for y in cl]
    rt = [x * w for x, w in zip(r, w_to)]
    kt = [x * w for x, w in zip(k, w_inv)]
    at = [x * w for x, w in zip(a, w_prev)]
    bt = [x * w for x, w in zip(b, w_inv)]
    bh = [x * w for x, w in zip(b, w_rem)]
    kh = [x * w for x, w in zip(k, w_rem)]
    lhs = [jnp.concatenate([x, y], axis=0).astype(BF16) for x, y in zip(at, rt)]
    rhs = [jnp.concatenate([_per_head(x.astype(BF16), m0, m1), _per_head(y.astype(BF16), m0, m1)], axis=0)
           for x, y in zip(bt, kt)]
    vb = [x.astype(BF16) for x in v]
    bk_t = [jnp.concatenate([x, y], axis=0).T.astype(BF16) for x, y in zip(bh, kh)]
    rkb = [(x * y * z).astype(BF16) for x, y, z in zip(r, k, rk_rows)]
    return list(zip(lhs, rhs, vb, at, rt, bk_t, w_all, rkb))


def _rwkv_chunk_maps(ops, cst):
    row, strict, incl, bd, eye, m0, m1, m0w, m1w, eye_c, merge_masks = cst
    c = CHUNK
    lhs, rhs, vb, at, rt, bk_t, w_all, _ = ([o[i] for o in ops] for i in range(8))
    per_head = _per_head
    a_all = [_dot_nt(x, y) for x, y in zip(lhs, rhs)]
    n_ab = [jnp.where(strict, x[:c, :PAIR], 0.0) for x in a_all]
    a_ak = [jnp.where(strict, x[:c, PAIR:], 0.0) for x in a_all]
    a_rb = [jnp.where(incl, x[c:, :PAIR], 0.0) for x in a_all]
    a_rk = [jnp.where(incl, x[c:, PAIR:], 0.0) for x in a_all]

    vs = [per_head(x, m0, m1) for x in vb]
    x0 = [jnp.concatenate([y, _dot(z.astype(BF16), w)], axis=1) for y, z, w in zip(at, a_ak, vs)]

    nb = [x.astype(BF16) for x in n_ab]
    t_inv = [eye_c + x * merge_masks[0] for x in n_ab]
    anchors = [None] * len(ops)
    for lvl, msk in enumerate(merge_masks[1:]):
        sz = 2 << lvl
        tb = [t.astype(BF16) for t in t_inv]
        if sz % 16:
            tmp = [_dot(t, per_head(n * msk, m0, m1)) for t, n in zip(tb, nb)]
            t_inv = [t + _dot(m.astype(BF16), per_head(h, m0, m1)) for t, m, h in zip(t_inv, tmp, tb)]
        else:
            low = [lo for lo in range(0, c, sz) if (lo // sz) % 2]
            pick = lambda y: jnp.concatenate([y[lo:lo + sz] for lo in low], axis=0)
            tmp = [_dot(pick(t), per_head(n * msk, m0, m1)) for t, n in zip(tb, nb)]
            upd = [_dot(m.astype(BF16), per_head(h, m0, m1)) for m, h in zip(tmp, tb)]

            def put_back(t, u):
                blocks = [t[lo:lo + sz] for lo in range(0, c, sz)]
                for j, lo in enumerate(low):
                    blocks[lo // sz] = blocks[lo // sz] + u[j * sz:(j + 1) * sz]
                return jnp.concatenate(blocks, axis=0)

            t_inv = [put_back(t, u) for t, u in zip(t_inv, upd)]
        for n in range(len(ops)):
            if lvl == (n * (len(merge_masks) - 1)) // len(ops):
                anchors[n] = tmp[n][:8]
    x1 = [_dot(t.astype(BF16), per_head(y.astype(BF16), m0w, m1w)) for t, y in zip(t_inv, x0)]
    x1b = [x.astype(BF16) for x in x1]
    rx = [_dot(z.astype(BF16), per_head(y, m0w, m1w)) for z, y in zip(a_rb, x1b)]
    r_hat = [x + y[:, :PAIR] for x, y in zip(rt, rx)]
    y_hat = [y[:, PAIR:] + _dot(z.astype(BF16), w) for y, z, w in zip(rx, a_rk, vs)]
    zero = jnp.zeros((c, PAIR), BF16)
    gm = [_dot(x, jnp.concatenate([y, jnp.concatenate([zero, z], axis=1)], axis=0))
          for x, y, z in zip(bk_t, x1b, vb)]
    m_mat = [eye * w + x[:, :PAIR] * bd for w, x in zip(w_all, gm)]
    s_hat = [x[:, PAIR:] * bd for x in gm]
    return list(zip(r_hat, y_hat, m_mat, s_hat)), anchors


def _rwkv_kernel(r_ref, k_ref, v_ref, ld_ref, a_ref, b_ref, g_ref, rk_ref, lw_ref, lb_ref,
                 bd_ref, avg2_ref, stat2_ref, y_ref, s_ref, *prep_refs):
    t = pl.program_id(1)

    @pl.when(t == 0)
    def _():
        s_ref[...] = jnp.zeros_like(s_ref)

    c = CHUNK
    lane = lax.broadcasted_iota(jnp.int32, (1, PAIR), 1)
    m0 = (lane < HEAD).astype(F32).astype(BF16)
    m1 = (lane >= HEAD).astype(F32).astype(BF16)
    lane_w = lax.broadcasted_iota(jnp.int32, (1, 2 * PAIR), 1) % PAIR
    m0w = (lane_w < HEAD).astype(F32).astype(BF16)
    m1w = (lane_w >= HEAD).astype(F32).astype(BF16)
    row = lax.broadcasted_iota(jnp.int32, (c, PAIR), 0)
    col = lax.broadcasted_iota(jnp.int32, (c, PAIR), 1) % c
    strict = col < row
    incl = col <= row
    bd = bd_ref[...]
    ri = lax.broadcasted_iota(jnp.int32, (PAIR, PAIR), 0)
    ci = lax.broadcasted_iota(jnp.int32, (PAIR, PAIR), 1)
    eye = (ri == ci).astype(F32)
    eye_c = (col == row).astype(F32)
    merge_masks = []
    sz = 1
    while sz < c:
        msk = (row // (2 * sz) == col // (2 * sz)) & (row % (2 * sz) >= sz) & (col % (2 * sz) < sz)
        merge_masks.append(msk.astype(F32) if sz == 1 else msk.astype(F32).astype(BF16))
        sz *= 2
    cst = (row, strict, incl, bd, eye, m0, m1, m0w, m1w, eye_c, merge_masks)

    n_pairs = D_RWKV // PAIR
    span = UNIT_CHUNKS * c
    units = [(ch, p) for ch in range(UNIT_CHUNKS) for p in range(n_pairs)]
    lanes = [slice(p * PAIR, (p + 1) * PAIR) for _, p in units]
    n_blocks = r_ref.shape[0] // span
    slot_a, slot_b = prep_refs[:len(prep_refs) // 2], prep_refs[len(prep_refs) // 2:]

    def unit_rows(blk):
        base = pl.multiple_of(blk * span, span)
        return [pl.ds(base + ch * c, c) for ch, _ in units]

    def exact_zero(x):
        bits = pltpu.bitcast(x, jnp.uint32)
        half = jnp.uint32(16)
        return pltpu.bitcast(lax.shift_right_logical(lax.shift_right_logical(bits, half), half), F32)

    def prep_block(blk, slot, after=None):
        slabs = [tuple(ref[rs, ls].astype(F32) for ref in (r_ref, k_ref, v_ref, ld_ref, a_ref, b_ref))
                 for rs, ls in zip(unit_rows(blk), lanes)]
        if after is not None:
            tied = [jnp.concatenate([s[3][:8] + exact_zero(x), s[3][8:]], axis=0) for s, x in zip(slabs, after)]
            slabs = [s[:3] + (ld,) + s[4:] for s, ld in zip(slabs, tied)]
        ops = _rwkv_prep(slabs, [rk_ref[:, ls] for ls in lanes], cst)
        for n, op in enumerate(ops):
            for ref, val in zip(slot, op):
                ref[n] = val

    def main_block(blk, slot):
        ops = [tuple(ref[n] for ref in slot) for n in range(len(units))]
        maps, anchors = _rwkv_chunk_maps(ops, cst)

        state = [s_ref[p] for p in range(n_pairs)]
        ys = []
        for (ch, p), (r_hat, y_hat, m_mat, s_hat) in zip(units, maps):
            sb = state[p].astype(BF16)
            ym = _dot(jnp.concatenate([r_hat, m_mat], axis=0).astype(BF16), sb)
            ys.append(ym[:c] + y_hat)
            state[p] = ym[c:] + s_hat
        for p in range(n_pairs):
            s_ref[p] = state[p]

        y_all = jnp.concatenate(ys, axis=0)
        dlt = y_all - _dot(jnp.concatenate(_split_bf16(y_all, 2), axis=1), avg2_ref[...])
        rkb = jnp.concatenate([op[7] for op in ops], axis=0)
        stats = _dot(jnp.concatenate([(dlt * dlt).astype(BF16), rkb], axis=1), stat2_ref[...])
        yn = dlt * lax.rsqrt(stats[:, :PAIR] + LNX_EPS)
        bonus = stats[:, PAIR:]
        for n, (op, rs, ls) in enumerate(zip(ops, unit_rows(blk), lanes)):
            un = slice(n * c, (n + 1) * c)
            y = yn[un] * lw_ref[:, ls] + lb_ref[:, ls] + bonus[un] * op[2].astype(F32)
            y_ref[rs, ls] = (y * g_ref[rs, ls].astype(F32)).astype(y_ref.dtype)
        return anchors

    prep_block(0, slot_a)

    per_trip = RWKV_BLOCKS_PER_TRIP

    def body(j, carry):
        for i in range(0, per_trip, 2):
            first = per_trip * j + i
            done = main_block(first, slot_a)
            prep_block(first + 1, slot_b, after=done)
            done = main_block(first + 1, slot_b)
            prep_block(jnp.minimum(first + 2, n_blocks - 1), slot_a, after=done)
        return carry

    lax.fori_loop(0, n_blocks // per_trip, body, 0)


def _rwkv(r, k, v, ld, a, b, g, r_k, lnx_w, lnx_b, tb):
    bsz, seq, d = r.shape
    tok = pl.BlockSpec((None, tb, d), lambda i, t: (i, t, 0))
    const = lambda shape: pl.BlockSpec(shape, lambda i, t: (0,) * len(shape))
    assert (tb // (UNIT_CHUNKS * CHUNK)) % 2 == 0, "the block pipeline is written for an even block count"
    c, nu = CHUNK, UNIT_CHUNKS * (d // PAIR)
    slot = [pltpu.VMEM((nu, 2 * c, PAIR), BF16), pltpu.VMEM((nu, 4 * c, PAIR), BF16),
            pltpu.VMEM((nu, c, PAIR), BF16), pltpu.VMEM((nu, c, PAIR), F32), pltpu.VMEM((nu, c, PAIR), F32),
            pltpu.VMEM((nu, PAIR, 2 * c), BF16), pltpu.VMEM((nu, 1, PAIR), F32), pltpu.VMEM((nu, c, PAIR), BF16)]
    avg = _head_block_diag(PAIR, BF16, 1.0 / HEAD)
    ones = _head_block_diag(PAIR, BF16)
    zero = jnp.zeros((PAIR, PAIR), BF16)
    avg2 = jnp.concatenate([avg, avg], axis=0)
    stat2 = jnp.concatenate([jnp.concatenate([avg, zero], axis=1), jnp.concatenate([zero, ones], axis=1)], axis=0)
    return pl.pallas_call(
        _rwkv_kernel,
        grid=(bsz, seq // tb),
        in_specs=[tok] * 7 + [const((1, d))] * 3
        + [const((PAIR, PAIR)), const((2 * PAIR, PAIR)), const((2 * PAIR, 2 * PAIR))],
        out_specs=tok,
        out_shape=jax.ShapeDtypeStruct((bsz, seq, d), BF16),
        scratch_shapes=[pltpu.VMEM((d // PAIR, PAIR, PAIR), F32)] + slot + slot,
        compiler_params=pltpu.CompilerParams(dimension_semantics=("parallel", "arbitrary"),
                                             vmem_limit_bytes=VMEM_LIMIT),
    )(r, k, v, ld, a, b, g, r_k.reshape(1, d), lnx_w.reshape(1, d), lnx_b.reshape(1, d),
      _head_block_diag(PAIR, F32), avg2, stat2)


def _s5_prep_kernel(are_ref, aim_ref, ldt_ref, btre_ref, btim_ref, cre_ref, cim_ref, dsk_ref,
                    qt_out, p_out, t_out, lam_out):
    for gi in range(are_ref.shape[0]):
        _s5_prep_group(are_ref[gi], aim_ref[gi], ldt_ref[gi], btre_ref[gi], btim_ref[gi], cre_ref[gi],
                       cim_ref[gi], dsk_ref[gi], qt_out.at[gi], p_out.at[gi], t_out.at[gi], lam_out.at[gi])


def _s5_prep_group(a_re, a_im, log_dt, bt_re, bt_im, c_re, c_im, d_tile, qt_out, p_out, t_out, lam_out):
    dt = jnp.exp(log_dt)
    nd = 24
    dpow = lax.broadcasted_iota(jnp.int32, (nd, S5_STATE), 0).astype(F32)
    mag = jnp.exp(dpow * (dt * a_re))
    ang = dpow * (dt * a_im)
    e_re = mag * jnp.cos(ang)
    e_im = mag * jnp.sin(ang)
    lam_re = e_re[1:2, :]
    lam_im = e_im[1:2, :]
    den = a_re * a_re + a_im * a_im
    pp = lam_re - 1.0
    qq = lam_im
    coef_re = (pp * a_re + qq * a_im) / den
    coef_im = (qq * a_re - pp * a_im) / den
    bb_re = coef_re * bt_re - coef_im * bt_im
    bb_im = coef_re * bt_im + coef_im * bt_re
    c_lam = []
    for d in range(S5_L + 1):
        er = e_re[d:d + 1, :]
        ei = e_im[d:d + 1, :]
        c_lam.append(jnp.concatenate([c_re * er - c_im * ei, -(c_re * ei + c_im * er)], axis=1))
    qt_out[...] = jnp.concatenate(c_lam[1:], axis=0)
    for j in range(S5_L):
        d = S5_L - 1 - j
        er = e_re[d:d + 1, :]
        ei = e_im[d:d + 1, :]
        p_out[j * S5_CH:(j + 1) * S5_CH, :] = jnp.concatenate(
            [er * bb_re - ei * bb_im, er * bb_im + ei * bb_re], axis=1)
    y0 = jnp.concatenate([bb_re, bb_im], axis=1)
    lagged = jnp.concatenate([jnp.concatenate([jnp.zeros_like(c_lam[0])] * j + c_lam[:S5_L - j], axis=0)
                              for j in range(S5_L)], axis=0)
    gw = S5_L * S5_CH
    wide = lax.dot_general(y0, lagged, (((1,), (1,)), ((), ())), preferred_element_type=F32,
                           precision=lax.Precision.HIGHEST)
    t_rows = [wide[:, j * gw:(j + 1) * gw] for j in range(S5_L)]
    on_diag = lax.broadcasted_iota(jnp.int32, (gw, gw), 0) == lax.broadcasted_iota(jnp.int32, (gw, gw), 1)
    t_out[...] = jnp.concatenate(t_rows, axis=0) + jnp.where(on_diag, d_tile, 0.0)
    lam_out[...] = jnp.concatenate([e_re[S5_L:S5_L + 1, :], e_im[S5_L:S5_L + 1, :]], axis=1)


def _s5_prep(a_re, a_im, log_dt, b_re, b_im, c_re, c_im, d_skip):
    g, p = a_re.shape
    per_g = lambda shape: pl.BlockSpec((S5_QG,) + shape, lambda i: (i,) + (0,) * len(shape))
    gw = S5_L * S5_CH
    return pl.pallas_call(
        _s5_prep_kernel,
        grid=(g // S5_QG,),
        in_specs=[per_g((1, p)), per_g((1, p)), per_g((1, 1)), per_g((S5_CH, p)), per_g((S5_CH, p)),
                  per_g((S5_CH, p)), per_g((S5_CH, p)), per_g((1, gw))],
        out_specs=[per_g((gw, 2 * p)), per_g((gw, 2 * p)), per_g((gw, gw)), per_g((1, 2 * p))],
        out_shape=[jax.ShapeDtypeStruct((g, gw, 2 * p), F32),
                   jax.ShapeDtypeStruct((g, gw, 2 * p), F32),
                   jax.ShapeDtypeStruct((g, gw, gw), F32),
                   jax.ShapeDtypeStruct((g, 1, 2 * p), F32)],
        compiler_params=pltpu.CompilerParams(dimension_semantics=("arbitrary",)),
    )(a_re.reshape(g, 1, p), a_im.reshape(g, 1, p), log_dt.reshape(g, 1, 1),
      jnp.swapaxes(b_re, 1, 2), jnp.swapaxes(b_im, 1, 2), c_re, c_im,
      jnp.tile(d_skip, (1, S5_L)).reshape(g, 1, gw))


def _s5_weights_kernel(p_ref, t_ref, qt_ref, perm_ref, bp_out, bt_out, bq_out, rows_ok_ref):
    gw = p_ref.shape[1]
    for h in range(p_ref.shape[0]):
        cols = slice(h * gw, (h + 1) * gw)
        perm_h = perm_ref[:, cols]
        bp_out[:, cols] = _dot(perm_h, p_ref[h].astype(BF16)).astype(BF16)
        rows_ok_ref[:, cols] = _dot(perm_h, t_ref[h].astype(BF16)).astype(BF16)
        bq_out[cols, :] = _dot_nt(qt_ref[h].T.astype(BF16), perm_h).astype(BF16)
    bt_out[...] = _dot_nt(rows_ok_ref[...], perm_ref[...]).astype(BF16)


def _s5_weights(p_all, t_all, qt_all):
    g, gw, sw = p_all.shape
    dst = jnp.arange(S5_W)
    j, h, a = dst // LANES, (dst % LANES) // S5_CH, dst % S5_CH
    src = h * gw + j * S5_CH + a
    perm = (src[:, None] == jnp.arange(S5_W)[None, :]).astype(BF16)
    grp = lambda n: pl.BlockSpec((S5_QG, gw, n), lambda q: (q, 0, 0))
    big = lambda r, c: pl.BlockSpec((None, r, c), lambda q: (q, 0, 0))
    return pl.pallas_call(
        _s5_weights_kernel,
        grid=(S5_NQ,),
        in_specs=[grp(sw), grp(gw), grp(sw), pl.BlockSpec((S5_W, S5_W), lambda q: (0, 0))],
        out_specs=[big(S5_W, S5_SW), big(S5_W, S5_W), big(S5_SW, S5_W)],
        out_shape=[jax.ShapeDtypeStruct((S5_NQ, S5_W, S5_SW), BF16),
                   jax.ShapeDtypeStruct((S5_NQ, S5_W, S5_W), BF16),
                   jax.ShapeDtypeStruct((S5_NQ, S5_SW, S5_W), BF16)],
        scratch_shapes=[pltpu.VMEM((S5_W, S5_W), BF16)],
        compiler_params=pltpu.CompilerParams(dimension_semantics=("arbitrary",),
                                             vmem_limit_bytes=VMEM_LIMIT),
    )(p_all, t_all, qt_all, perm)


def _s5_kernel(u_ref, bp_ref, bt_ref, bq_ref, lr_ref, li_ref, y_ref, z_ref, s_ref):
    @pl.when(pl.program_id(0) == 0)
    def _():
        s_ref[...] = jnp.zeros_like(s_ref)

    nb, ct, _ = u_ref.shape
    rows = nb * ct
    kb = S5_SW // LANES

    def cols(q):
        return [slice(tl * D_S5 + q * LANES, tl * D_S5 + (q + 1) * LANES) for tl in range(S5_L)]

    def x_of(q):
        return jnp.concatenate([u_ref[:, :, cs].reshape(rows, LANES) for cs in cols(q)], axis=1)

    for q in range(S5_NQ):
        z = _dot(x_of(q), bp_ref[q])
        z_ref[:, q * kb:(q + 1) * kb, :] = z.reshape(rows, kb, LANES)

    for q in range(S5_NQ):
        gs = slice(q * kb, (q + 1) * kb)
        pick = lambda ref: jnp.concatenate([ref[b * S5_SROWS + q * kb:b * S5_SROWS + (q + 1) * kb] for b in range(nb)], axis=0)
        lr, li = pick(lr_ref), pick(li_ref)
        s = pick(s_ref)
        s_sw = pltpu.roll(s, S5_STATE, axis=1)
        for c in range(ct):
            z = jnp.concatenate([z_ref[b * ct + c, gs, :] for b in range(nb)], axis=0)
            for b in range(nb):
                z_ref[b * ct + c, gs, :] = s[b * kb:(b + 1) * kb]
            z_sw = pltpu.roll(z, S5_STATE, axis=1)
            s, s_sw = lr * s + li * s_sw + z, lr * s_sw - li * s + z_sw
        for b in range(nb):
            s_ref[b * S5_SROWS + q * kb:b * S5_SROWS + (q + 1) * kb] = s[b * kb:(b + 1) * kb]

    for q in range(S5_NQ):
        start = z_ref[:, q * kb:(q + 1) * kb, :].reshape(rows, kb * LANES).astype(BF16)
        x = x_of(q)
        pair = 2 * LANES
        local = jnp.concatenate([_dot(x[:, :(i + 1) * pair], bt_ref[q, :(i + 1) * pair, i * pair:(i + 1) * pair])
                                 for i in range(S5_W // pair)], axis=1)
        y = (local + _dot(start, bq_ref[q])).astype(y_ref.dtype)
        for tl, cs in enumerate(cols(q)):
            y_ref[:, :, cs] = y[:, tl * LANES:(tl + 1) * LANES].reshape(nb, ct, LANES)


def _s5_core(u2, a_re, a_im, log_dt, b_re, b_im, c_re, c_im, d_skip, ct):
    bsz, nck, _ = u2.shape
    p, ll, nq = S5_STATE, S5_L, S5_NQ
    qt_all, p_all, t_all, lam = _s5_prep(a_re, a_im, log_dt, b_re, b_im, c_re, c_im, d_skip)
    big_p, big_t, big_q = _s5_weights(p_all, t_all, qt_all)
    lam_re = lam[:, 0, :p]
    lam_im = lam[:, 0, p:]
    lr = jnp.tile(jnp.concatenate([lam_re, lam_re], axis=1), (bsz, 1))
    li = jnp.tile(jnp.concatenate([-lam_im, lam_im], axis=1), (bsz, 1))

    once = dict(pipeline_mode=pl.Buffered(1))
    const = lambda shape: pl.BlockSpec(shape, lambda i: (0,) * len(shape), **once)
    tok = pl.BlockSpec((bsz, ct, ll * D_S5), lambda i: (0, i, 0))
    y2 = pl.pallas_call(
        _s5_kernel,
        grid=(nck // ct,),
        in_specs=[tok, const((nq, S5_W, S5_SW)), const((nq, S5_W, S5_W)), const((nq, S5_SW, S5_W)),
                  const((bsz * S5_SROWS, LANES)), const((bsz * S5_SROWS, LANES))],
        out_specs=tok,
        out_shape=jax.ShapeDtypeStruct((bsz, nck, ll * D_S5), BF16),
        scratch_shapes=[pltpu.VMEM((bsz * ct, S5_SROWS, LANES), F32),
                        pltpu.VMEM((bsz * S5_SROWS, LANES), F32)],
        compiler_params=pltpu.CompilerParams(dimension_semantics=("arbitrary",),
                                             vmem_limit_bytes=VMEM_LIMIT),
    )(u2, big_p, big_t, big_q, lr, li)
    return y2


def _tail_kernel(x_ref, yr_ref, ys_ref, gm_ref, shf_ref, scf_ref, gf_ref, wglu_ref, bglu_ref, gain_ref,
                 wo_ref, wg_ref, wu_ref, wd_ref, fg_ref, o_ref, ys_scr, act_scr, *, ff_tile):
    tm = x_ref.shape[0]
    for tl in range(S5_L):
        for k in range(S5_NQ):
            lo = tl * D_S5 + k * LANES
            ys_scr[k, pl.ds(tl, tm // S5_L, stride=S5_L), :] = ys_ref[:, lo:lo + LANES].astype(F32)
    ys = jnp.concatenate([ys_scr[k] for k in range(S5_NQ)], axis=1)
    zz = 0.5 * ys * (1.0 + jnp.tanh(math.sqrt(2.0 / math.pi) * (ys + 0.044715 * (ys * ys * ys))))
    gl = zz * _sigmoid(_dot(zz.astype(BF16), wglu_ref[...]) + bglu_ref[...])
    gl = gl * lax.rsqrt(jnp.mean(gl * gl, axis=-1, keepdims=True) + NORM_EPS) * gain_ref[...]
    mix = _dot(yr_ref[...].astype(BF16), wo_ref[:D_RWKV, :]) + _dot(gl.astype(BF16), wo_ref[D_RWKV:, :])
    x1 = x_ref[...] + gm_ref[...] * mix
    h = x1 * lax.rsqrt(jnp.mean(x1 * x1, axis=-1, keepdims=True) + NORM_EPS)
    h = (h * (1.0 + scf_ref[...]) + shf_ref[...]).astype(BF16)
    for j in range(D_FF // ff_tile):
        cs = slice(j * ff_tile, (j + 1) * ff_tile)
        gate = _dot(h, wg_ref[:, cs])
        up = _dot(h, wu_ref[:, cs])
        act_scr[:, cs] = (gate * _sigmoid(gate) * up).astype(BF16)
    x2 = x1 + gf_ref[...] * _dot(act_scr[...], wd_ref[...])
    o_ref[...] = x2 * lax.rsqrt(jnp.mean(x2 * x2, axis=-1, keepdims=True) + NORM_EPS) * fg_ref[...]


def _tail(x, y_rwkv, y_s5, g_m, sh_f, sc_f, g_f, w_glu, b_glu, gain, w_out, w_gate, w_up, w_down,
          final_gain, tm, ff_tile):
    bsz, seq, d = x.shape
    once = dict(pipeline_mode=pl.Buffered(1))
    const = lambda shape: pl.BlockSpec(shape, lambda b, t: (0,) * len(shape), **once)
    tok = lambda n: pl.BlockSpec((None, tm, n), lambda b, t: (b, t, 0))
    per_b = pl.BlockSpec((None, 1, d), lambda b, t: (b, 0, 0))
    b3 = lambda a: a.reshape(bsz, 1, d)
    return pl.pallas_call(
        functools.partial(_tail_kernel, ff_tile=ff_tile),
        grid=(bsz, seq // tm),
        in_specs=[tok(d), tok(D_RWKV), pl.BlockSpec((None, tm // S5_L, S5_L * D_S5), lambda b, t: (b, t, 0)),
                  per_b, per_b, per_b, per_b,
                  const((D_S5, D_S5)), const((1, D_S5)), const((1, D_S5)),
                  const((D_RWKV + D_S5, d)), const((d, D_FF)), const((d, D_FF)), const((D_FF, d)),
                  const((1, d))],
        out_specs=tok(d),
        out_shape=jax.ShapeDtypeStruct((bsz, seq, d), F32),
        scratch_shapes=[pltpu.VMEM((S5_NQ, tm, LANES), F32), pltpu.VMEM((tm, D_FF), BF16)],
        compiler_params=pltpu.CompilerParams(dimension_semantics=("parallel", "parallel"),
                                             vmem_limit_bytes=VMEM_LIMIT),
    )(x, y_rwkv, y_s5, b3(g_m), b3(sh_f), b3(sc_f), b3(g_f), w_glu.astype(BF16), b_glu.reshape(1, -1),
      gain.reshape(1, -1), w_out.astype(BF16), w_gate.astype(BF16), w_up.astype(BF16),
      w_down.astype(BF16), final_gain.reshape(1, d))


def kernel(x, c, w_ada, b_ada, w_in, mu_shift, rw_w0, rw_w2, rw_a0, rw_a2, rw_g2, rw_k_k, rw_k_a, rw_r_k,
           rw_lnx_w, rw_lnx_b, s5_a_re, s5_a_im, s5_log_dt, s5_b_re, s5_b_im, s5_c_re, s5_c_im, s5_d,
           s5_w_glu, s5_b_glu, s5_gain, w_out, ffn_w_gate, ffn_w_up, ffn_w_down, final_gain):
    assert w_ada.shape[0] == 1, "single-layer trunk"
    seq = x.shape[1]
    ada = _ada(c, w_ada[0], b_ada[0])
    sh_m, sc_m, g_m, sh_f, sc_f, g_f = jnp.split(ada, 6, axis=-1)
    r, k, v, ld, a, b, g, u = _inproj(x, sh_m, sc_m, w_in[0], mu_shift[0], rw_w0[0], rw_w2[0], rw_a0[0],
                                      rw_a2[0], rw_g2[0], rw_k_k[0], rw_k_a[0], tm=min(INPROJ_TILE, seq))
    y_rwkv = _rwkv(r, k, v, ld, a, b, g, rw_r_k[0], rw_lnx_w[0], rw_lnx_b[0], tb=min(RWKV_TILE, seq))
    y_s5 = _s5_core(u, s5_a_re[0], s5_a_im[0], s5_log_dt[0], s5_b_re[0], s5_b_im[0], s5_c_re[0],
                    s5_c_im[0], s5_d[0], ct=min(S5_TILE, seq // S5_L))
    return _tail(x, y_rwkv, y_s5, g_m, sh_f, sc_f, g_f, s5_w_glu[0], s5_b_glu[0], s5_gain[0], w_out[0],
                 ffn_w_gate[0], ffn_w_up[0], ffn_w_down[0], final_gain, tm=min(TAIL_TILE, seq),
                 ff_tile=FF_TILE)
```
